```python
import math
import jax, jax.numpy as jnp
from jax import lax
import numpy as np

D_MODEL = 1024
BATCH = 8
SEQ = 2048
DEPTH = 2
DEC_BATCH = 32
DEC_SEQ = 1
PAST_LEN = 8192
PAGE_SIZE = 128

N_A_LAYERS = DEPTH // 2
N_B_LAYERS = DEPTH - N_A_LAYERS
RET_HEADS = D_MODEL // 256
RET_DK = D_MODEL // RET_HEADS
RET_DV = 2 * RET_DK
RET_CHUNK = 128
ROPE_BASE = 10000.0
FOX_HEADS = D_MODEL // 64
FOX_HD = 64
Q_BLOCK = 128
N_GROUPS = 4
EXPERTS_PER_GROUP = 8
N_EXPERTS = N_GROUPS * EXPERTS_PER_GROUP
TOP_K_INNER = 2
D_EXPERT = D_MODEL // 4
EPS = 1e-6
NEG_INF = -1e30

kernel_name = 'yoco_retnet_fox_hmoe_step'


def rmsnorm(x, g):
    xf = x.astype(jnp.float32)
    y = xf * lax.rsqrt(jnp.mean(xf * xf, axis=-1, keepdims=True) + EPS)
    return (y * g.astype(jnp.float32)).astype(x.dtype)


def adaln(c, w, b, n):
    m = jax.nn.silu(c) @ w + b
    return jnp.split(m[:, None, :], n, axis=-1)


def modulate(h, shift, scale):
    return h * (1.0 + scale) + shift


def rotary(x, pos):
    half = x.shape[-1] // 2
    inv = ROPE_BASE ** (-jnp.arange(half, dtype=jnp.float32) / half)
    ang = pos.astype(jnp.float32)[:, None] * inv[None, :]
    cos = jnp.cos(ang)[None, :, None, :]
    sin = jnp.sin(ang)[None, :, None, :]
    x1, x2 = x[..., :half], x[..., half:]
    return jnp.concatenate([x1 * cos - x2 * sin, x1 * sin + x2 * cos], axis=-1)


def retention_log_gamma():
    return jnp.log(1.0 - 2.0 ** (-5.0 - jnp.arange(RET_HEADS, dtype=jnp.float32)))


def retention_chunk(S, q, k, v, log_gamma):
    L = q.shape[1]
    t = jnp.arange(L, dtype=jnp.float32)
    rel = t[:, None] - t[None, :]
    decay = jnp.where(rel >= 0, jnp.exp(log_gamma[:, None, None] * jnp.maximum(rel, 0.0)), 0.0)
    a = jnp.einsum('bihd,bjhd->bhij', q, k) * decay[None]
    inner = jnp.einsum('bhij,bjhv->bihv', a, v)
    q_dec = jnp.exp((t[:, None] + 1.0) * log_gamma[None, :])
    cross = jnp.einsum('bihd,bhdv->bihv', q * q_dec[None, :, :, None], S)
    k_dec = jnp.exp((L - 1.0 - t)[:, None] * log_gamma[None, :])
    S_new = jnp.exp(L * log_gamma)[None, :, None, None] * S + jnp.einsum('bjhd,bjhv->bhdv', k * k_dec[None, :, :, None], v)
    return inner + cross, S_new


def retention_mixer(h, pos, S0, w_in, w_out):
    B, L, _ = h.shape
    qd = RET_HEADS * RET_DK
    vd = RET_HEADS * RET_DV
    proj = h @ w_in
    q, k, v, g = jnp.split(proj, [qd, 2 * qd, 2 * qd + vd], axis=-1)
    q = rotary(q.reshape(B, L, RET_HEADS, RET_DK).astype(jnp.float32), pos)
    k = rotary(k.reshape(B, L, RET_HEADS, RET_DK).astype(jnp.float32), pos) * (RET_DK ** -0.5)
    v = v.reshape(B, L, RET_HEADS, RET_DV).astype(jnp.float32)
    lg = retention_log_gamma()
    if S0 is None:
        S0 = jnp.zeros((B, RET_HEADS, RET_DK, RET_DV), jnp.float32)
    else:
        S0 = S0.astype(jnp.float32)
    if L > RET_CHUNK and L % RET_CHUNK == 0:
        nc = L // RET_CHUNK

        def to_chunks(t):
            return jnp.moveaxis(t.reshape(B, nc, RET_CHUNK, *t.shape[2:]), 1, 0)

        def step(S, qkv):
            o, S = retention_chunk(S, qkv[0], qkv[1], qkv[2], lg)
            return S, o

        S, o = lax.scan(step, S0, (to_chunks(q), to_chunks(k), to_chunks(v)))
        o = jnp.moveaxis(o, 0, 1).reshape(B, L, RET_HEADS, RET_DV)
    else:
        o, S = retention_chunk(S0, q, k, v, lg)
    o = o * lax.rsqrt(jnp.mean(o * o, axis=-1, keepdims=True) + EPS)
    o = o.reshape(B, L, vd).astype(h.dtype) * jax.nn.silu(g)
    return o @ w_out, S


def fox_block(q, k, v, Fq, Fk, q_pos, k_pos):
    s = jnp.einsum('bqhd,bkhd->bhqk', q, k).astype(jnp.float32) * (FOX_HD ** -0.5)
    s = s + jnp.swapaxes(Fq, 1, 2)[..., :, None] - jnp.swapaxes(Fk, 1, 2)[..., None, :]
    s = jnp.where((k_pos[None, :] <= q_pos[:, None])[None, None], s, NEG_INF)
    p = jax.nn.softmax(s, axis=-1)
    return jnp.einsum('bhqk,bkhd->bqhd', p.astype(v.dtype), v)


def fox_attention(q, k, v, Fq, Fk, q_pos, k_pos):
    B, Lq = q.shape[0], q.shape[1]
    if Lq > Q_BLOCK and Lq % Q_BLOCK == 0:
        nb = Lq // Q_BLOCK
        qb = jnp.moveaxis(q.reshape(B, nb, Q_BLOCK, *q.shape[2:]), 1, 0)
        Fqb = jnp.moveaxis(Fq.reshape(B, nb, Q_BLOCK, Fq.shape[-1]), 1, 0)
        pb = q_pos.reshape(nb, Q_BLOCK)
        out = lax.map(lambda a: fox_block(a[0], k, v, a[1], Fk, a[2], k_pos), (qb, Fqb, pb))
        return jnp.moveaxis(out, 0, 1).reshape(q.shape)
    return fox_block(q, k, v, Fq, Fk, q_pos, k_pos)


def fox_mixer(h, shared, pos, w_q, w_o):
    k_all, v_all, F = shared
    B, L, _ = h.shape
    Lk = k_all.shape[1]
    q = (h @ w_q).reshape(B, L, FOX_HEADS, FOX_HD)
    o = fox_attention(q, k_all, v_all, F[:, Lk - L:], F, pos, jnp.arange(Lk))
    return o.reshape(B, L, FOX_HEADS * FOX_HD) @ w_o


def build_shared_kv(x, c, past, p):
    B, L, _ = x.shape
    sh, sc = adaln(c, p['w_kvmod'], p['b_kvmod'], 2)
    n = modulate(rmsnorm(x, p['g_kv']), sh, sc)
    k, v = jnp.split(n @ p['w_kv'], 2, axis=-1)
    k = k.reshape(B, L, FOX_HEADS, FOX_HD)
    v = v.reshape(B, L, FOX_HEADS, FOX_HD)
    logf = jax.nn.log_sigmoid((n @ p['w_f']).astype(jnp.float32) + p['b_f'].astype(jnp.float32))
    if past is None:
        k_all, v_all, lf_all = k, v, logf
    else:
        k_all = jnp.concatenate([past[0].astype(k.dtype), k], axis=1)
        v_all = jnp.concatenate([past[1].astype(v.dtype), v], axis=1)
        lf_all = jnp.concatenate([past[2].astype(jnp.float32), logf], axis=1)
    F = jnp.cumsum(lf_all, axis=1)
    return (k, v, logf), (k_all, v_all, F)


def hier_moe(h, w_rg, w_re, w_gate, w_up, w_down):
    B, L, _ = h.shape
    lg = (h @ w_rg).astype(jnp.float32)
    pg = jax.nn.softmax(lg, axis=-1)
    _, gi = lax.top_k(lg, 1)
    g_onehot = jax.nn.one_hot(gi[..., 0], N_GROUPS, dtype=jnp.float32)
    p_group = jnp.sum(pg * g_onehot, axis=-1, keepdims=True)
    le = (h @ w_re).astype(jnp.float32).reshape(B, L, N_GROUPS, EXPERTS_PER_GROUP)
    le_sel = jnp.einsum('blge,blg->ble', le, g_onehot)
    tv, ti = lax.top_k(le_sel, TOP_K_INNER)
    w_sel = jax.nn.softmax(tv, axis=-1) * p_group
    ids = gi * EXPERTS_PER_GROUP + ti
    gates = jnp.einsum('blk,blke->ble', w_sel, jax.nn.one_hot(ids, N_EXPERTS, dtype=jnp.float32))
    a = jax.nn.silu(h @ w_gate) * (h @ w_up)
    a = (a.reshape(B, L, N_EXPERTS, D_EXPERT) * gates[..., None].astype(a.dtype)).reshape(B, L, N_EXPERTS * D_EXPERT)
    return a @ w_down


def trunk(x, c, ret_state, past, p):
    B, L, _ = x.shape
    P = 0 if past is None else past[0].shape[1]
    pos = P + jnp.arange(L)
    new_ret = []
    kv_rows = None
    shared = None
    for l in range(DEPTH):
        sh1, sc1, g1, sh2, sc2, g2 = adaln(c, p['w_mod'][l], p['b_mod'][l], 6)
        if l < N_A_LAYERS:
            h = modulate(rmsnorm(x, p['g_mix'][l]), sh1, sc1)
            s0 = None if ret_state is None else ret_state[l]
            o, s_new = retention_mixer(h, pos, s0, p['w_ret_in'][l], p['w_ret_out'][l])
            new_ret.append(s_new)
        else:
            if l == N_A_LAYERS:
                kv_rows, shared = build_shared_kv(x, c, past, p)
            h = modulate(rmsnorm(x, p['g_mix'][l]), sh1, sc1)
            o = fox_mixer(h, shared, pos, p['w_fq'][l - N_A_LAYERS], p['w_fo'][l - N_A_LAYERS])
        x = x + g1 * o
        h = modulate(rmsnorm(x, p['g_ffn'][l]), sh2, sc2)
        x = x + g2 * hier_moe(h, p['w_rg'][l], p['w_re'][l], p['w_e_gate'][l], p['w_e_up'][l], p['w_e_down'][l])
    y = rmsnorm(x, p['g_final'])
    return y, jnp.stack(new_ret, axis=0), kv_rows


def setup_inputs(seed: int = 0) -> dict:
    key = jax.random.key(seed)
    ks = jax.random.split(key, 32)
    D = D_MODEL
    n_pages = PAST_LEN // PAGE_SIZE
    n_used = DEC_BATCH * n_pages
    n_pool = (n_used * 5) // 4
    qd = RET_HEADS * RET_DK
    vd = RET_HEADS * RET_DV
    fd = FOX_HEADS * FOX_HD
    ef = N_EXPERTS * D_EXPERT
    nrm = jax.random.normal
    page_table = jax.random.permutation(ks[9], n_pool)[:n_used].reshape(DEC_BATCH, n_pages).astype(jnp.int32)
    return {
        'x_prompt': nrm(ks[0], (BATCH, SEQ, D), jnp.float32),
        'x_sample': nrm(ks[1], (DEC_BATCH, DEC_SEQ, D), jnp.float32),
        'c_prompt': nrm(ks[2], (BATCH, D), jnp.float32),
        'c_sample': nrm(ks[3], (DEC_BATCH, D), jnp.float32),
        'state_ret': nrm(ks[4], (N_A_LAYERS, DEC_BATCH, RET_HEADS, RET_DK, RET_DV), jnp.float32) * (RET_DK ** -0.5),
        'cache_k': nrm(ks[5], (n_pool, PAGE_SIZE, FOX_HEADS, FOX_HD), jnp.float32),
        'cache_v': nrm(ks[6], (n_pool, PAGE_SIZE, FOX_HEADS, FOX_HD), jnp.float32),
        'cache_logf': jax.nn.log_sigmoid(2.5 + nrm(ks[7], (n_pool, PAGE_SIZE, FOX_HEADS), jnp.float32)),
        'page_table': page_table,
        'w_mod': nrm(ks[10], (DEPTH, D, 6 * D), jnp.float32) * D ** -0.5,
        'b_mod': 0.01 * nrm(ks[11], (DEPTH, 6 * D), jnp.float32),
        'g_mix': 1.0 + 0.01 * nrm(ks[12], (DEPTH, D), jnp.float32),
        'g_ffn': 1.0 + 0.01 * nrm(ks[13], (DEPTH, D), jnp.float32),
        'w_ret_in': nrm(ks[14], (N_A_LAYERS, D, 2 * qd + 2 * vd), jnp.float32) * D ** -0.5,
        'w_ret_out': nrm(ks[15], (N_A_LAYERS, vd, D), jnp.float32) * vd ** -0.5,
        'g_kv': 1.0 + 0.01 * nrm(ks[16], (D,), jnp.float32),
        'w_kvmod': nrm(ks[17], (D, 2 * D), jnp.float32) * D ** -0.5,
        'b_kvmod': 0.01 * nrm(ks[18], (2 * D,), jnp.float32),
        'w_kv': nrm(ks[19], (D, 2 * fd), jnp.float32) * D ** -0.5,
        'w_f': nrm(ks[20], (D, FOX_HEADS), jnp.float32) * D ** -0.5,
        'b_f': jax.random.uniform(ks[21], (FOX_HEADS,), jnp.float32, 1.0, 4.0),
        'w_fq': nrm(ks[22], (N_B_LAYERS, D, fd), jnp.float32) * D ** -0.5,
        'w_fo': nrm(ks[23], (N_B_LAYERS, fd, D), jnp.float32) * fd ** -0.5,
        'w_rg': nrm(ks[24], (DEPTH, D, N_GROUPS), jnp.float32) * D ** -0.5,
        'w_re': nrm(ks[25], (DEPTH, D, N_EXPERTS), jnp.float32) * D ** -0.5,
        'w_e_gate': nrm(ks[26], (DEPTH, D, ef), jnp.float32) * D ** -0.5,
        'w_e_up': nrm(ks[27], (DEPTH, D, ef), jnp.float32) * D ** -0.5,
        'w_e_down': nrm(ks[28], (DEPTH, ef, D), jnp.float32) * D_EXPERT ** -0.5,
        'g_final': 1.0 + 0.01 * nrm(ks[29], (D,), jnp.float32),
    }


def reference(x_prompt, x_sample, c_prompt, c_sample, state_ret, cache_k, cache_v, cache_logf, page_table,
              w_mod, b_mod, g_mix, g_ffn, w_ret_in, w_ret_out, g_kv, w_kvmod, b_kvmod, w_kv, w_f, b_f,
              w_fq, w_fo, w_rg, w_re, w_e_gate, w_e_up, w_e_down, g_final):
    p = dict(w_mod=w_mod, b_mod=b_mod, g_mix=g_mix, g_ffn=g_ffn, w_ret_in=w_ret_in, w_ret_out=w_ret_out,
             g_kv=g_kv, w_kvmod=w_kvmod, b_kvmod=b_kvmod, w_kv=w_kv, w_f=w_f, b_f=b_f, w_fq=w_fq, w_fo=w_fo,
             w_rg=w_rg, w_re=w_re, w_e_gate=w_e_gate, w_e_up=w_e_up, w_e_down=w_e_down, g_final=g_final)
    y_prompt, state_ret_prompt, kv_p = trunk(x_prompt, c_prompt, None, None, p)
    nb, n_pages = page_table.shape
    past_len = n_pages * PAGE_SIZE
    past_k = cache_k[page_table].reshape(nb, past_len, FOX_HEADS, FOX_HD)
    past_v = cache_v[page_table].reshape(nb, past_len, FOX_HEADS, FOX_HD)
    past_lf = cache_logf[page_table].reshape(nb, past_len, FOX_HEADS)
    y_sample, state_ret_sample, kv_s = trunk(x_sample, c_sample, state_ret, (past_k, past_v, past_lf), p)
    k_prompt, v_prompt, logf_prompt = kv_p
    k_sample, v_sample, logf_sample = kv_s
    return (y_prompt, y_sample, state_ret_prompt, state_ret_sample, k_prompt, v_prompt, logf_prompt, k_sample, v_sample, logf_sample)
```

```python
import functools
import math

import jax
import jax.numpy as jnp
from jax import lax
from jax.experimental import pallas as pl
from jax.experimental.pallas import tpu as pltpu

D_MODEL = 1024
RET_HEADS = 4
RET_DK = 256
RET_DV = 512
ROPE_BASE = 10000.0
FOX_HEADS = 16
FOX_HD = 64
N_GROUPS = 4
EXPERTS_PER_GROUP = 8
N_EXPERTS = 32
D_EXPERT = 256
EPS = 1e-6
NEG_INF = -1e30

LANES = 128
VMEM_LIMIT = 56 * 1024 * 1024
F32 = jnp.float32
BF16 = jnp.bfloat16


def _cparams(n_axes):
    return pltpu.CompilerParams(dimension_semantics=("arbitrary",) * n_axes,
                                vmem_limit_bytes=VMEM_LIMIT)


def _silu(x):
    return x * jax.nn.sigmoid(x)


def _dot(a, b):
    return jnp.dot(a, b, preferred_element_type=F32)


def _dot_nt(a, b):
    return lax.dot_general(a, b, (((1,), (1,)), ((), ())), preferred_element_type=F32)


def _dot_tn(a, b):
    return lax.dot_general(a, b, (((0,), (0,)), ((), ())), preferred_element_type=F32)


def _mm(a, w, precise):
    if precise:
        return jnp.dot(a, w, preferred_element_type=F32, precision=lax.Precision.HIGHEST)
    return _dot(a.astype(BF16), w)


def _split3(x):
    hi = x.astype(BF16)
    r1 = x - hi.astype(F32)
    mid = r1.astype(BF16)
    lo = (r1 - mid.astype(F32)).astype(BF16)
    return hi, mid, lo


def _rms_mod(x, g, shift, scale):
    y = x * lax.rsqrt(jnp.mean(x * x, axis=-1, keepdims=True) + EPS)
    return (y * g) * (1.0 + scale) + shift


def _mod_kernel(c_ref, w_ref, b_ref, o_ref):
    o_ref[...] = _mm(_silu(c_ref[...]), w_ref[...], True) + b_ref[...]


def _mod_call(c, w, b, tn):
    ns, k, n = w.shape
    m = c.shape[0]
    return pl.pallas_call(
        _mod_kernel,
        grid=(ns, n // tn),
        in_specs=[
            pl.BlockSpec((m, k), lambda s, j: (0, 0)),
            pl.BlockSpec((None, k, tn), lambda s, j: (s, 0, j)),
            pl.BlockSpec((None, 1, tn), lambda s, j: (s, 0, j)),
        ],
        out_specs=pl.BlockSpec((None, m, tn), lambda s, j: (s, 0, j)),
        out_shape=jax.ShapeDtypeStruct((ns, m, n), F32),
        compiler_params=_cparams(2),
        name="adaln_mod",
    )(c, w, b.reshape(ns, 1, n))


class _Mod:
    def __init__(self, arr, per_token, row0, tm):
        self.per_token = per_token
        self.row0 = row0
        self.tm = tm
        self.arr = arr if per_token else arr.reshape(arr.shape[0], arr.shape[1], 1, arr.shape[2])

    def spec(self, stack, col):
        if self.per_token:
            assert self.row0 % self.tm == 0
            r0 = self.row0 // self.tm
            return pl.BlockSpec((None, self.tm, D_MODEL), lambda b, i: (stack, r0 + i, col))
        row0 = self.row0
        return pl.BlockSpec((None, None, 1, D_MODEL), lambda b, i: (stack, row0 + b, 0, col))


def _tok_spec(tm, width):
    return pl.BlockSpec((None, tm, width), lambda b, i: (b, i, 0))


def _const_spec(shape):
    nd = len(shape)
    return pl.BlockSpec(shape, lambda b, i: (0,) * nd)


def _retin_kernel(x_ref, sh_ref, sc_ref, g_ref, w_ref, cos_ref, sin_ref, o_ref, h_ref, *, precise):
    j = pl.program_id(2)

    @pl.when(j == 0)
    def _():
        h_ref[...] = _rms_mod(x_ref[...], g_ref[...], sh_ref[...], sc_ref[...]).astype(h_ref.dtype)

    p = _mm(h_ref[...], w_ref[...], precise)

    @pl.when(j < 2)
    def _():
        scale = jnp.where(j == 0, 1.0, RET_DK ** -0.5)
        half = RET_DK // 2
        cos = cos_ref[...]
        sin = sin_ref[...]
        for hh in range(RET_HEADS):
            a = hh * RET_DK
            x1 = p[:, a:a + half]
            x2 = p[:, a + half:a + RET_DK]
            o_ref[:, a:a + half] = ((x1 * cos - x2 * sin) * scale).astype(o_ref.dtype)
            o_ref[:, a + half:a + RET_DK] = ((x1 * sin + x2 * cos) * scale).astype(o_ref.dtype)

    @pl.when(j >= 2)
    def _():
        o_ref[...] = p.astype(o_ref.dtype)


def _retin_call(x, mod, g_mix, w_in, cos, sin, tm, precise):
    bx, lx, d = x.shape
    tn = RET_HEADS * RET_DK
    n = w_in.shape[-1]
    dt = F32 if precise else BF16
    spec3 = lambda sp: pl.BlockSpec(sp.block_shape, lambda b, i, j: sp.index_map(b, i))
    return pl.pallas_call(
        functools.partial(_retin_kernel, precise=precise),
        grid=(bx, lx // tm, n // tn),
        in_specs=[
            spec3(_tok_spec(tm, d)),
            spec3(mod.spec(0, 0)), spec3(mod.spec(0, 1)),
            spec3(_const_spec((1, d))),
            pl.BlockSpec((None, d, tn), lambda b, i, j: (0, 0, j)),
            pl.BlockSpec((tm, RET_DK // 2), lambda b, i, j: (i, 0)),
            pl.BlockSpec((tm, RET_DK // 2), lambda b, i, j: (i, 0)),
        ],
        out_specs=pl.BlockSpec((None, tm, tn), lambda b, i, j: (b, i, j)),
        out_shape=jax.ShapeDtypeStruct((bx, lx, n), dt),
        scratch_shapes=[pltpu.VMEM((tm, d), dt)],
        compiler_params=_cparams(3),
        name="ret_in_proj",
    )(x, mod.arr, mod.arr, g_mix, w_in, cos, sin)


def _log_gamma(h):
    return math.log(1.0 - 2.0 ** (-5.0 - h))


def _ret_prompt_kernel(q_ref, k_ref, v_ref, g_ref, o_ref, s_ref, *, chunk):
    @pl.when(pl.program_id(1) == 0)
    def _():
        s_ref[...] = jnp.zeros_like(s_ref)

    ti = lax.broadcasted_iota(jnp.int32, (chunk, chunk), 0)
    tj = lax.broadcasted_iota(jnp.int32, (chunk, chunk), 1)
    rel = (ti - tj).astype(F32)
    t = lax.broadcasted_iota(jnp.int32, (chunk, 1), 0).astype(F32)
    for h in range(RET_HEADS):
        lg = _log_gamma(h)
        decay = jnp.where(rel >= 0, jnp.exp(lg * jnp.maximum(rel, 0.0)), 0.0)
        qh = q_ref[:, h * RET_DK:(h + 1) * RET_DK]
        kh = k_ref[:, h * RET_DK:(h + 1) * RET_DK]
        vh = v_ref[:, h * RET_DV:(h + 1) * RET_DV]
        a = _dot_nt(qh, kh) * decay
        inner = _dot(a.astype(BF16), vh)
        s_old = s_ref[h]
        cross = _dot(qh, s_old.astype(BF16)) * jnp.exp((t + 1.0) * lg)
        o = inner + cross
        kd = (kh.astype(F32) * jnp.exp((chunk - 1.0 - t) * lg)).astype(BF16)
        s_ref[h] = math.exp(chunk * lg) * s_old + _dot_tn(kd, vh)
        o = o * lax.rsqrt(jnp.mean(o * o, axis=-1, keepdims=True) + EPS)
        gh = g_ref[:, h * RET_DV:(h + 1) * RET_DV].astype(F32)
        o_ref[:, h * RET_DV:(h + 1) * RET_DV] = (o * _silu(gh)).astype(BF16)


def _ret_prompt_call(proj, chunk):
    b, l, _ = proj.shape
    qd = RET_HEADS * RET_DK
    vd = RET_HEADS * RET_DV
    col = lambda w, j: pl.BlockSpec((None, chunk, w), lambda bb, c: (bb, c, j))
    return pl.pallas_call(
        functools.partial(_ret_prompt_kernel, chunk=chunk),
        grid=(b, l // chunk),
        in_specs=[col(qd, 0), col(qd, 1), col(vd, 1), col(vd, 2)],
        out_specs=[_tok_spec(chunk, vd),
                   pl.BlockSpec((None, RET_HEADS, RET_DK, RET_DV), lambda bb, c: (bb, 0, 0, 0))],
        out_shape=[jax.ShapeDtypeStruct((b, l, vd), BF16),
                   jax.ShapeDtypeStruct((b, RET_HEADS, RET_DK, RET_DV), F32)],
        compiler_params=_cparams(2),
        name="retention_prompt",
    )(proj, proj, proj, proj)


def _ret_sample_kernel(q_ref, k_ref, v_ref, g_ref, s_ref, o_ref, sn_ref):
    b = pl.program_id(0)
    h = pl.program_id(1)
    nb = q_ref.shape[0]
    gamma = 1.0 - 1.0 / (jnp.zeros((1, 1), F32) + (jnp.int32(32) << h).astype(F32))
    rowsel = lax.broadcasted_iota(jnp.int32, (nb, 1), 0) == b
    q = jnp.where(rowsel, q_ref[...], 0.0)
    k = jnp.where(rowsel, k_ref[...], 0.0)
    v = v_ref[...]
    vb = jnp.sum(jnp.where(rowsel, v, 0.0), axis=0, keepdims=True)
    gb = jnp.sum(jnp.where(rowsel, g_ref[...], 0.0), axis=0, keepdims=True)
    s_old = s_ref[...]
    kv = lax.dot_general(k, v, (((0,), (0,)), ((), ())), preferred_element_type=F32,
                         precision=lax.Precision.HIGHEST)
    sn_ref[...] = gamma * s_old + kv
    qk = jnp.sum(jnp.sum(q * k, axis=1, keepdims=True), axis=0, keepdims=True)
    cross = jnp.sum(_mm(q, s_old, True), axis=0, keepdims=True) * gamma
    o = qk * vb + cross
    o = o * lax.rsqrt(jnp.mean(o * o, axis=-1, keepdims=True) + EPS)
    o_ref[...] = o * _silu(gb)


def _ret_sample_call(proj, s0):
    nb = proj.shape[0]
    k0 = RET_HEADS
    v0 = 2 * RET_HEADS * RET_DK // RET_DV
    g0 = v0 + RET_HEADS
    return pl.pallas_call(
        _ret_sample_kernel,
        grid=(nb, RET_HEADS),
        in_specs=[
            pl.BlockSpec((nb, RET_DK), lambda b, h: (0, h)),
            pl.BlockSpec((nb, RET_DK), lambda b, h: (0, k0 + h)),
            pl.BlockSpec((nb, RET_DV), lambda b, h: (0, v0 + h)),
            pl.BlockSpec((nb, RET_DV), lambda b, h: (0, g0 + h)),
            pl.BlockSpec((None, None, RET_DK, RET_DV), lambda b, h: (b, h, 0, 0)),
        ],
        out_specs=[pl.BlockSpec((None, 1, RET_DV), lambda b, h: (b, 0, h)),
                   pl.BlockSpec((None, None, RET_DK, RET_DV), lambda b, h: (b, h, 0, 0))],
        out_shape=[jax.ShapeDtypeStruct((nb, 1, RET_HEADS * RET_DV), F32),
                   jax.ShapeDtypeStruct((nb, RET_HEADS, RET_DK, RET_DV), F32)],
        compiler_params=_cparams(2),
        name="retention_sample",
    )(proj, proj, proj, proj, s0)


def _route(logits):
    tm = logits.shape[0]
    lane = lax.broadcasted_iota(jnp.int32, (tm, LANES), 1).astype(F32)
    ninf = -jnp.inf
    lg = jnp.where(lane < N_GROUPS, logits[:, :LANES], ninf)
    mx = jnp.max(lg, axis=-1, keepdims=True)
    p_group = 1.0 / jnp.sum(jnp.exp(lg - mx), axis=-1, keepdims=True)
    gi = jnp.min(jnp.where(lg == mx, lane, float(LANES)), axis=-1, keepdims=True)
    lo = gi * EXPERTS_PER_GROUP
    les = jnp.where((lane >= lo) & (lane < lo + EXPERTS_PER_GROUP), logits[:, LANES:], ninf)
    v1 = jnp.max(les, axis=-1, keepdims=True)
    i1 = jnp.min(jnp.where(les == v1, lane, float(LANES)), axis=-1, keepdims=True)
    les2 = jnp.where(lane == i1, ninf, les)
    v2 = jnp.max(les2, axis=-1, keepdims=True)
    i2 = jnp.min(jnp.where(les2 == v2, lane, float(LANES)), axis=-1, keepdims=True)
    e2 = jnp.exp(v2 - v1)
    w1 = 1.0 / (1.0 + e2)
    w2 = e2 / (1.0 + e2)
    return jnp.where(lane == i1, w1 * p_group, 0.0) + jnp.where(lane == i2, w2 * p_group, 0.0)


def _mixout_kernel(o_ref, w_ref, x_ref, g1_ref, sh_ref, sc_ref, gf_ref, wr_ref, x1_ref, h2_ref, gates_ref, *,
                   precise):
    y = _mm(o_ref[...], w_ref[...], precise)
    x1 = x_ref[...] + g1_ref[...] * y
    x1_ref[...] = x1
    h2 = _rms_mod(x1, gf_ref[...], sh_ref[...], sc_ref[...])
    h2_ref[...] = h2.astype(h2_ref.dtype)
    gates_ref[...] = _route(_mm(h2, wr_ref[...], True))


def _mixout_call(o, w_out, x, mod, layer, g_ffn, w_router, tm, precise):
    bx, lx, d = x.shape
    kd = o.shape[-1]
    return pl.pallas_call(
        functools.partial(_mixout_kernel, precise=precise),
        grid=(bx, lx // tm),
        in_specs=[
            _tok_spec(tm, kd),
            _const_spec((kd, d)),
            _tok_spec(tm, d),
            mod.spec(layer, 2), mod.spec(layer, 3), mod.spec(layer, 4),
            _const_spec((1, d)),
            _const_spec((d, 2 * LANES)),
        ],
        out_specs=[_tok_spec(tm, d), _tok_spec(tm, d), _tok_spec(tm, LANES)],
        out_shape=[jax.ShapeDtypeStruct((bx, lx, d), F32),
                   jax.ShapeDtypeStruct((bx, lx, d), F32 if precise else BF16),
                   jax.ShapeDtypeStruct((bx, lx, LANES), F32)],
        compiler_params=_cparams(2),
        name="mix_out_router",
    )(o, w_out, x, mod.arr, mod.arr, mod.arr, g_ffn, w_router)


def _moe_kernel(h_ref, gates_ref, wg_ref, wu_ref, wd_ref, x1_ref, g2_ref, gfin_ref, out_ref, acc_ref, *, final,
                precise):
    e = pl.program_id(2)

    @pl.when(e == 0)
    def _():
        acc_ref[...] = jnp.zeros_like(acc_ref)

    h = h_ref[...]
    a = _mm(h, wg_ref[...], precise)
    u = _mm(h, wu_ref[...], precise)
    lane = lax.broadcasted_iota(jnp.int32, gates_ref.shape, 1)
    ge = jnp.sum(jnp.where(lane == e, gates_ref[...], 0.0), axis=-1, keepdims=True)
    act = (_silu(a) * u) * ge
    acc_ref[...] += _mm(act, wd_ref[...], precise)

    @pl.when(e == pl.num_programs(2) - 1)
    def _():
        x2 = x1_ref[...] + g2_ref[...] * acc_ref[...]
        if final:
            x2 = (x2 * lax.rsqrt(jnp.mean(x2 * x2, axis=-1, keepdims=True) + EPS)) * gfin_ref[...]
        out_ref[...] = x2


def _moe_call(h2, gates, wg, wu, wd, layer, x1, mod, g_final, tm, final, precise):
    bx, lx, d = x1.shape
    tok3 = lambda w: pl.BlockSpec((None, tm, w), lambda b, i, e: (b, i, 0))
    g2_spec2 = mod.spec(layer, 5)
    g2_spec = pl.BlockSpec(g2_spec2.block_shape, lambda b, i, e: g2_spec2.index_map(b, i))
    return pl.pallas_call(
        functools.partial(_moe_kernel, final=final, precise=precise),
        grid=(bx, lx // tm, N_EXPERTS),
        in_specs=[
            tok3(d), tok3(LANES),
            pl.BlockSpec((None, d, D_EXPERT), lambda b, i, e: (layer, 0, e)),
            pl.BlockSpec((None, d, D_EXPERT), lambda b, i, e: (layer, 0, e)),
            pl.BlockSpec((None, D_EXPERT, d), lambda b, i, e: (layer, e, 0)),
            tok3(d),
            g2_spec,
            pl.BlockSpec((1, d), lambda b, i, e: (0, 0)),
        ],
        out_specs=tok3(d),
        out_shape=jax.ShapeDtypeStruct((bx, lx, d), F32),
        scratch_shapes=[pltpu.VMEM((tm, d), F32)],
        compiler_params=_cparams(3),
        name="hier_moe",
    )(h2, gates, wg, wu, wd, x1, mod.arr, g_final)


def _log_sigmoid(z):
    return -(jnp.maximum(-z, 0.0) + jnp.log1p(jnp.exp(-jnp.abs(z))))


def _kvq_kernel(x_ref, shk_ref, sck_ref, gkv_ref, wkv_ref, wf_ref, bf_ref, sh1_ref, sc1_ref, gmix_ref, wq_ref,
                k32_ref, v32_ref, k16_ref, v16_ref, lf_ref, q_ref, *, precise):
    x = x_ref[...]
    fd = FOX_HEADS * FOX_HD
    n = _rms_mod(x, gkv_ref[...], shk_ref[...], sck_ref[...])
    if not precise:
        n = n.astype(BF16)
    k = _mm(n, wkv_ref[:, 0:fd], precise)
    k32_ref[...] = k
    k16_ref[...] = k.astype(BF16)
    v = _mm(n, wkv_ref[:, fd:2 * fd], precise)
    v32_ref[...] = v
    v16_ref[...] = v.astype(BF16)
    z = _mm(n, wf_ref[...], precise) + bf_ref[...]
    lf_ref[...] = _log_sigmoid(z)[:, :FOX_HEADS]
    h = _rms_mod(x, gmix_ref[...], sh1_ref[...], sc1_ref[...])
    q_ref[...] = (_mm(h, wq_ref[...], precise) * (FOX_HD ** -0.5)).astype(q_ref.dtype)


def _kvq_call(x, kvmod, mod, g_kv, w_kv, w_f, b_f, g_mix, w_q, tm, precise):
    bx, lx, d = x.shape
    fd = FOX_HEADS * FOX_HD
    return pl.pallas_call(
        functools.partial(_kvq_kernel, precise=precise),
        grid=(bx, lx // tm),
        in_specs=[
            _tok_spec(tm, d),
            kvmod.spec(0, 0), kvmod.spec(0, 1),
            _const_spec((1, d)),
            _const_spec((d, 2 * fd)),
            _const_spec((d, LANES)),
            _const_spec((1, LANES)),
            mod.spec(1, 0), mod.spec(1, 1),
            _const_spec((1, d)),
            _const_spec((d, fd)),
        ],
        out_specs=[_tok_spec(tm, fd), _tok_spec(tm, fd), _tok_spec(tm, fd), _tok_spec(tm, fd),
                   _tok_spec(tm, FOX_HEADS), _tok_spec(tm, fd)],
        out_shape=[jax.ShapeDtypeStruct((bx, lx, fd), F32), jax.ShapeDtypeStruct((bx, lx, fd), F32),
                   jax.ShapeDtypeStruct((bx, lx, fd), BF16), jax.ShapeDtypeStruct((bx, lx, fd), BF16),
                   jax.ShapeDtypeStruct((bx, lx, FOX_HEADS), F32),
                   jax.ShapeDtypeStruct((bx, lx, fd), F32 if precise else BF16)],
        compiler_params=_cparams(2),
        name="kv_q_proj",
    )(x, kvmod.arr, kvmod.arr, g_kv, w_kv, w_f, b_f, mod.arr, mod.arr, g_mix, w_q)


def _cumsum_kernel(x_ref, o_ref):
    nh, l = x_ref.shape
    r = lax.broadcasted_iota(jnp.int32, (LANES, LANES), 0)
    c = lax.broadcasted_iota(jnp.int32, (LANES, LANES), 1)
    upper = (r <= c).astype(BF16)
    carry = jnp.zeros((nh, 1), F32)
    for blk in range(l // LANES):
        hi, mid, lo = _split3(x_ref[:, blk * LANES:(blk + 1) * LANES])
        cs = (_dot(hi, upper) + _dot(mid, upper)) + _dot(lo, upper) + carry
        o_ref[:, blk * LANES:(blk + 1) * LANES] = cs
        carry = cs[:, LANES - 1:LANES]


def _cumsum_call(x):
    b, nh, l = x.shape
    return pl.pallas_call(
        _cumsum_kernel,
        grid=(b,),
        in_specs=[pl.BlockSpec((None, nh, l), lambda i: (i, 0, 0))],
        out_specs=pl.BlockSpec((None, nh, l), lambda i: (i, 0, 0)),
        out_shape=jax.ShapeDtypeStruct((b, nh, l), F32),
        compiler_params=_cparams(1),
        name="logf_cumsum",
    )(x)


def _fox_kernel(q_ref, k_ref, v_ref, fq_ref, fk_ref, o_ref, acc_a, acc_b, *, tq, tk):
    qi = pl.program_id(2)
    lane = lax.broadcasted_iota(jnp.int32, (1, LANES), 1)
    first = lane < FOX_HD
    q2 = q_ref[...].astype(BF16)
    zero = jnp.zeros_like(q2)
    q_a = jnp.where(first, q2, zero)
    q_b = jnp.where(first, zero, q2)
    fq_a = fq_ref[:, 0:1]
    fq_b = fq_ref[:, 1:2]
    row = qi * tq + lax.broadcasted_iota(jnp.int32, (tq, tk), 0)
    col = lax.broadcasted_iota(jnp.int32, (tq, tk), 1)
    acc_a[...] = jnp.zeros_like(acc_a)
    acc_b[...] = jnp.zeros_like(acc_b)

    def head(qm, fq, fk, kt, vt, mask, m, l, acc):
        s = (_dot_nt(qm, kt) + fq) - fk
        s = jnp.where(mask, s, NEG_INF)
        m_new = jnp.maximum(m, jnp.max(s, axis=-1, keepdims=True))
        p = jnp.exp(s - m_new)
        alpha = jnp.exp(m - m_new)
        acc[...] = alpha * acc[...] + _dot(p.astype(BF16), vt)
        return m_new, alpha * l + jnp.sum(p, axis=-1, keepdims=True)

    def body(j, carry):
        m_a, l_a, m_b, l_b = carry
        start = pl.multiple_of(j * tk, tk)
        kt = k_ref[pl.ds(start, tk), :]
        vt = v_ref[pl.ds(start, tk), :]
        fk = fk_ref[j]
        mask = (col + j * tk) <= row
        m_a, l_a = head(q_a, fq_a, fk[0:1, :], kt, vt, mask, m_a, l_a, acc_a)
        m_b, l_b = head(q_b, fq_b, fk[1:2, :], kt, vt, mask, m_b, l_b, acc_b)
        return m_a, l_a, m_b, l_b

    m0 = jnp.full((tq, 1), -jnp.inf, F32)
    l0 = jnp.zeros((tq, 1), F32)
    n_kt = ((qi + 1) * tq + tk - 1) // tk
    _, l_a, _, l_b = lax.fori_loop(0, n_kt, body, (m0, l0, m0, l0))
    o_ref[...] = jnp.where(first, acc_a[...] / l_a, acc_b[...] / l_b).astype(BF16)


def _fox_prompt_call(q, k16, v16, fq, fk, tq, tk):
    b, l, fd = q.shape
    npair = fd // LANES
    return pl.pallas_call(
        functools.partial(_fox_kernel, tq=tq, tk=tk),
        grid=(b, npair, l // tq),
        in_specs=[
            pl.BlockSpec((None, tq, LANES), lambda bb, hp, i: (bb, i, hp)),
            pl.BlockSpec((None, l, LANES), lambda bb, hp, i: (bb, 0, hp)),
            pl.BlockSpec((None, l, LANES), lambda bb, hp, i: (bb, 0, hp)),
            pl.BlockSpec((None, None, tq, 2), lambda bb, hp, i: (bb, hp, i, 0)),
            pl.BlockSpec((None, None, l // tk, 2, tk), lambda bb, hp, i: (bb, hp, 0, 0, 0)),
        ],
        out_specs=pl.BlockSpec((None, tq, LANES), lambda bb, hp, i: (bb, i, hp)),
        out_shape=jax.ShapeDtypeStruct((b, l, fd), BF16),
        scratch_shapes=[pltpu.VMEM((tq, LANES), F32), pltpu.VMEM((tq, LANES), F32)],
        compiler_params=_cparams(3),
        name="fox_attention_prompt",
    )(q, k16, v16, fq, fk)


def _attn_sample_kernel(pt_ref, q_ref, kn_ref, vn_ref, lfn_ref, *refs, pps, page):
    ck = refs[0:pps]
    cv = refs[pps:2 * pps]
    clf = refs[2 * pps:3 * pps]
    o_ref = refs[3 * pps]
    m_ref, l_ref, acc_ref, carry_ref = refs[3 * pps + 1:]
    step = pl.program_id(1)
    fd = FOX_HEADS * FOX_HD
    hrow = lax.broadcasted_iota(jnp.int32, (FOX_HEADS, fd), 0)
    hlane = lax.broadcasted_iota(jnp.int32, (FOX_HEADS, fd), 1) // FOX_HD
    diag = hrow == hlane
    q = q_ref[...].astype(F32)
    q_bd = jnp.where(diag, jnp.broadcast_to(q, (FOX_HEADS, fd)), 0.0)

    @pl.when(step == 0)
    def _():
        m_ref[...] = jnp.sum(q_bd * kn_ref[...], axis=-1, keepdims=True)
        l_ref[...] = jnp.ones_like(l_ref)
        acc_ref[...] = jnp.broadcast_to(vn_ref[...], (FOX_HEADS, fd))
        r16 = lax.broadcasted_iota(jnp.int32, (FOX_HEADS, FOX_HEADS), 0)
        c16 = lax.broadcasted_iota(jnp.int32, (FOX_HEADS, FOX_HEADS), 1)
        carry_ref[...] = jnp.sum(jnp.where(r16 == c16, jnp.broadcast_to(lfn_ref[...], (FOX_HEADS, FOX_HEADS)), 0.0),
                                 axis=-1, keepdims=True)

    tt = lax.broadcasted_iota(jnp.int32, (page, 2 * page), 0)
    jj = lax.broadcasted_iota(jnp.int32, (page, 2 * page), 1)
    later = ((tt > jj) | (jj >= page)).astype(BF16)
    nh = FOX_HEADS

    def split2(x):
        hi = x.astype(BF16)
        return jnp.concatenate([hi, (x - hi.astype(F32)).astype(BF16)], axis=0)

    q2 = split2(q_bd)
    for i in range(pps):
        hi, mid, lo = _split3(clf[i][...])
        lfx = (_dot(hi, later) + _dot(mid, later)) + _dot(lo, later)
        carry = carry_ref[...]
        bias = lfx[:, :page] + carry
        carry_ref[...] = carry + lfx[:, page:page + 1]
        s2 = _dot_nt(q2, ck[i][...].astype(BF16))
        s = (s2[:nh] + s2[nh:]) + bias
        m_old = m_ref[...]
        m_new = jnp.maximum(m_old, jnp.max(s, axis=-1, keepdims=True))
        p = jnp.exp(s - m_new)
        alpha = jnp.exp(m_old - m_new)
        m_ref[...] = m_new
        l_ref[...] = alpha * l_ref[...] + jnp.sum(p, axis=-1, keepdims=True)
        pv = _dot(split2(p), cv[i][...].astype(BF16))
        acc_ref[...] = alpha * acc_ref[...] + (pv[:nh] + pv[nh:])

    @pl.when(step == pl.num_programs(1) - 1)
    def _():
        o = acc_ref[...] / l_ref[...]
        o_ref[...] = jnp.sum(jnp.where(diag, o, 0.0), axis=0, keepdims=True)


def _attn_sample_call(page_table, q, k_new, v_new, lf_new, cache_k, cache_v, cache_lf, pps):
    nb, n_pages = page_table.shape
    n_pool, page, fd = cache_k.shape
    nh = cache_lf.shape[1]
    steps = n_pages // pps

    def page_map(i):
        return lambda b, s, pt: (pt[b, n_pages - 1 - (s * pps + i)], 0, 0)

    row = lambda w: pl.BlockSpec((None, 1, w), lambda b, s, pt: (b, 0, 0))
    in_specs = [row(fd), row(fd), row(fd), row(nh)]
    in_specs += [pl.BlockSpec((None, page, fd), page_map(i)) for i in range(pps)]
    in_specs += [pl.BlockSpec((None, page, fd), page_map(i)) for i in range(pps)]
    in_specs += [pl.BlockSpec((None, nh, page), page_map(i)) for i in range(pps)]
    grid_spec = pltpu.PrefetchScalarGridSpec(
        num_scalar_prefetch=1,
        grid=(nb, steps),
        in_specs=in_specs,
        out_specs=row(fd),
        scratch_shapes=[pltpu.VMEM((nh, 1), F32), pltpu.VMEM((nh, 1), F32), pltpu.VMEM((nh, fd), F32),
                        pltpu.VMEM((nh, 1), F32)],
    )
    return pl.pallas_call(
        functools.partial(_attn_sample_kernel, pps=pps, page=page),
        grid_spec=grid_spec,
        out_shape=jax.ShapeDtypeStruct((nb, 1, fd), F32),
        compiler_params=_cparams(2),
        name="fox_attention_sample",
    )(page_table, q, k_new, v_new, lf_new, *([cache_k] * pps), *([cache_v] * pps), *([cache_lf] * pps))


def _rope_tables(pos):
    half = RET_DK // 2
    inv = ROPE_BASE ** (-jnp.arange(half, dtype=F32) / half)
    ang = pos.astype(F32)[:, None] * inv[None, :]
    return jnp.cos(ang), jnp.sin(ang)


def _pick_tile(n, pref):
    t = min(n, pref)
    while n % t:
        t //= 2
    return t


def kernel(x_prompt, x_sample, c_prompt, c_sample, state_ret, cache_k, cache_v, cache_logf, page_table, w_mod, b_mod, g_mix, g_ffn, w_ret_in, w_ret_out, g_kv, w_kvmod, b_kvmod, w_kv, w_f, b_f, w_fq, w_fo, w_rg, w_re, w_e_gate, w_e_up, w_e_down, g_final):
    b, l, d = x_prompt.shape
    nb = x_sample.shape[0]
    n_pool, page = cache_k.shape[0], cache_k.shape[1]
    n_pages = page_table.shape[1]
    past_len = n_pages * page
    fd = FOX_HEADS * FOX_HD

    c_all = jnp.concatenate([c_sample, c_prompt], axis=0)
    mod_all = _mod_call(c_all, w_mod, b_mod, 1536)
    kvmod_all = _mod_call(c_all, w_kvmod[None], b_kvmod[None], 1024)

    w_f_pad = jnp.pad(w_f, ((0, 0), (0, LANES - FOX_HEADS)))
    weights32 = dict(w_in=w_ret_in, w_out=w_ret_out[0], w_kv=w_kv, w_fq=w_fq[0], w_fo=w_fo[0], w_f=w_f_pad,
                     wg=w_e_gate, wu=w_e_up, wd=w_e_down)
    weights16 = {name: w.astype(BF16) for name, w in weights32.items()}
    b_f_row = jnp.pad(b_f, (0, LANES - FOX_HEADS)).reshape(1, LANES)
    w_router = jnp.concatenate([jnp.pad(w_rg, ((0, 0), (0, 0), (0, LANES - N_GROUPS))),
                                jnp.pad(w_re, ((0, 0), (0, 0), (0, LANES - N_EXPERTS)))], axis=-1)
    g_mix2 = g_mix.reshape(-1, 1, d)
    g_ffn2 = g_ffn.reshape(-1, 1, d)
    g_kv2 = g_kv.reshape(1, d)
    g_fin2 = g_final.reshape(1, d)

    def trunk(x, mod, kvmod, pos, tm, tm_in, tm_moe, mixer0, mixer1, precise):
        w = weights32 if precise else weights16
        cos, sin = _rope_tables(pos)
        proj = _retin_call(x, mod, g_mix2[0], w["w_in"], cos, sin, tm_in, precise)
        o, s_new = mixer0(proj)
        x1, h2, gates = _mixout_call(o, w["w_out"], x, mod, 0, g_ffn2[0], w_router[0], tm, precise)
        x2 = _moe_call(h2, gates, w["wg"], w["wu"], w["wd"], 0, x1, mod, g_fin2, tm_moe, False, precise)
        k32, v32, k16, v16, lf, qf = _kvq_call(x2, kvmod, mod, g_kv2, w["w_kv"], w["w_f"], b_f_row, g_mix2[1],
                                               w["w_fq"], tm, precise)
        o1 = mixer1(qf, k32, v32, k16, v16, lf)
        x3, h4, gates1 = _mixout_call(o1, w["w_fo"], x2, mod, 1, g_ffn2[1], w_router[1], tm, precise)
        y = _moe_call(h4, gates1, w["wg"], w["wu"], w["wd"], 1, x3, mod, g_fin2, tm_moe, True, precise)
        return y, s_new, k32, v32, lf

    tm_p = _pick_tile(l, 512)
    chunk = _pick_tile(l, 256)
    tq = _pick_tile(l, 256)

    def ret_prompt(proj):
        return _ret_prompt_call(proj, chunk)

    def fox_prompt(qf, k32, v32, k16, v16, lf):
        f_t = _cumsum_call(jnp.swapaxes(lf, 1, 2))
        f_pairs = f_t.reshape(b, FOX_HEADS // 2, 2, l)
        fq = jnp.swapaxes(f_pairs, 2, 3)
        fk = jnp.swapaxes(f_pairs.reshape(b, FOX_HEADS // 2, 2, l // tq, tq), 2, 3)
        return _fox_prompt_call(qf, k16, v16, fq, fk, tq, tq)

    mod_p = _Mod(mod_all, False, nb, tm_p)
    kvmod_p = _Mod(kvmod_all, False, nb, tm_p)
    y_p, s_p, k_p, v_p, lf_p = trunk(x_prompt, mod_p, kvmod_p, jnp.arange(l), tm_p, _pick_tile(l, 1024),
                                     _pick_tile(l, 1024), ret_prompt, fox_prompt, False)

    def ret_sample(proj):
        o, s_new = _ret_sample_call(proj[0], state_ret[0])
        return o.reshape(1, nb, -1), s_new

    def fox_sample(qf, k32, v32, k16, v16, lf):
        ck = cache_k.reshape(n_pool, page, fd)
        cv = cache_v.reshape(n_pool, page, fd)
        o = _attn_sample_call(page_table, qf.reshape(nb, 1, fd), k32.reshape(nb, 1, fd), v32.reshape(nb, 1, fd),
                              lf.reshape(nb, 1, FOX_HEADS), ck, cv, jnp.swapaxes(cache_logf, 1, 2),
                              _pick_tile(n_pages, 4))
        return o.reshape(1, nb, fd)

    mod_s = _Mod(mod_all, True, 0, nb)
    kvmod_s = _Mod(kvmod_all, True, 0, nb)
    y_s, s_s, k_s, v_s, lf_s = trunk(x_sample.reshape(1, nb, d), mod_s, kvmod_s,
                                     jnp.full((nb,), past_len, jnp.int32), nb, nb, nb, ret_sample, fox_sample, True)

    return (y_p, y_s.reshape(nb, 1, d), s_p[None], s_s[None],
            k_p.reshape(b, l, FOX_HEADS, FOX_HD), v_p.reshape(b, l, FOX_HEADS, FOX_HD), lf_p,
            k_s.reshape(nb, 1, FOX_HEADS, FOX_HD), v_s.reshape(nb, 1, FOX_HEADS, FOX_HD),
            lf_s.reshape(nb, 1, FOX_HEADS))
```

```python
import functools
import math

import jax
import jax.numpy as jnp
from jax import lax
from jax.experimental import pallas as pl
from jax.experimental.pallas import tpu as pltpu

D_MODEL = 1024
RET_HEADS = 4
RET_DK = 256
RET_DV = 512
ROPE_BASE = 10000.0
FOX_HEADS = 16
FOX_HD = 64
N_GROUPS = 4
EXPERTS_PER_GROUP = 8
N_EXPERTS = 32
D_EXPERT = 256
EPS = 1e-6
NEG_INF = -1e30

LANES = 128
VMEM_LIMIT = 56 * 1024 * 1024
F32 = jnp.float32
BF16 = jnp.bfloat16


def _cparams(n_axes):
    return pltpu.CompilerParams(dimension_semantics=("arbitrary",) * n_axes,
                                vmem_limit_bytes=VMEM_LIMIT)


def _silu(x):
    return x * jax.nn.sigmoid(x)


def _dot(a, b):
    return jnp.dot(a, b, preferred_element_type=F32)


def _dot_nt(a, b):
    return lax.dot_general(a, b, (((1,), (1,)), ((), ())), preferred_element_type=F32)


def _dot_tn(a, b):
    return lax.dot_general(a, b, (((0,), (0,)), ((), ())), preferred_element_type=F32)


def _mm(a, w, precise):
    if precise:
        return jnp.dot(a, w, preferred_element_type=F32, precision=lax.Precision.HIGHEST)
    return _dot(a.astype(BF16), w)


def _split3(x):
    hi = x.astype(BF16)
    r1 = x - hi.astype(F32)
    mid = r1.astype(BF16)
    lo = (r1 - mid.astype(F32)).astype(BF16)
    return hi, mid, lo


def _rms_mod(x, g, shift, scale):
    y = x * lax.rsqrt(jnp.mean(x * x, axis=-1, keepdims=True) + EPS)
    return (y * g) * (1.0 + scale) + shift


def _mod_kernel(c_ref, w_ref, b_ref, o_ref):
    o_ref[...] = _mm(_silu(c_ref[...]), w_ref[...], True) + b_ref[...]


def _mod_call(c, w, b, tn):
    ns, k, n = w.shape
    m = c.shape[0]
    return pl.pallas_call(
        _mod_kernel,
        grid=(ns, n // tn),
        in_specs=[
            pl.BlockSpec((m, k), lambda s, j: (0, 0)),
            pl.BlockSpec((None, k, tn), lambda s, j: (s, 0, j)),
            pl.BlockSpec((None, 1, tn), lambda s, j: (s, 0, j)),
        ],
        out_specs=pl.BlockSpec((None, m, tn), lambda s, j: (s, 0, j)),
        out_shape=jax.ShapeDtypeStruct((ns, m, n), F32),
        compiler_params=_cparams(2),
        name="adaln_mod",
    )(c, w, b.reshape(ns, 1, n))


class _Mod:
    def __init__(self, arr, per_token, row0, tm):
        self.per_token = per_token
        self.row0 = row0
        self.tm = tm
        self.arr = arr if per_token else arr.reshape(arr.shape[0], arr.shape[1], 1, arr.shape[2])

    def spec(self, stack, col):
        if self.per_token:
            assert self.row0 % self.tm == 0
            r0 = self.row0 // self.tm
            return pl.BlockSpec((None, self.tm, D_MODEL), lambda b, i: (stack, r0 + i, col))
        row0 = self.row0
        return pl.BlockSpec((None, None, 1, D_MODEL), lambda b, i: (stack, row0 + b, 0, col))


def _tok_spec(tm, width):
    return pl.BlockSpec((None, tm, width), lambda b, i: (b, i, 0))


def _const_spec(shape):
    nd = len(shape)
    return pl.BlockSpec(shape, lambda b, i: (0,) * nd)


def _retin_kernel(x_ref, sh_ref, sc_ref, g_ref, w_ref, cos_ref, sin_ref, o_ref, h_ref, *, precise):
    j = pl.program_id(2)

    @pl.when(j == 0)
    def _():
        h_ref[...] = _rms_mod(x_ref[...], g_ref[...], sh_ref[...], sc_ref[...]).astype(h_ref.dtype)

    p = _mm(h_ref[...], w_ref[...], precise)

    @pl.when(j < 2)
    def _():
        scale = jnp.where(j == 0, 1.0, RET_DK ** -0.5)
        half = RET_DK // 2
        cos = cos_ref[...]
        sin = sin_ref[...]
        for hh in range(RET_HEADS):
            a = hh * RET_DK
            x1 = p[:, a:a + half]
            x2 = p[:, a + half:a + RET_DK]
            o_ref[:, a:a + half] = ((x1 * cos - x2 * sin) * scale).astype(o_ref.dtype)
            o_ref[:, a + half:a + RET_DK] = ((x1 * sin + x2 * cos) * scale).astype(o_ref.dtype)

    @pl.when(j >= 2)
    def _():
        o_ref[...] = p.astype(o_ref.dtype)


def _retin_call(x, mod, g_mix, w_in, cos, sin, tm, precise):
    bx, lx, d = x.shape
    tn = RET_HEADS * RET_DK
    n = w_in.shape[-1]
    dt = F32 if precise else BF16
    spec3 = lambda sp: pl.BlockSpec(sp.block_shape, lambda b, i, j: sp.index_map(b, i))
    return pl.pallas_call(
        functools.partial(_retin_kernel, precise=precise),
        grid=(bx, lx // tm, n // tn),
        in_specs=[
            spec3(_tok_spec(tm, d)),
            spec3(mod.spec(0, 0)), spec3(mod.spec(0, 1)),
            spec3(_const_spec((1, d))),
            pl.BlockSpec((None, d, tn), lambda b, i, j: (0, 0, j)),
            pl.BlockSpec((tm, RET_DK // 2), lambda b, i, j: (i, 0)),
            pl.BlockSpec((tm, RET_DK // 2), lambda b, i, j: (i, 0)),
        ],
        out_specs=pl.BlockSpec((None, tm, tn), lambda b, i, j: (b, i, j)),
        out_shape=jax.ShapeDtypeStruct((bx, lx, n), dt),
        scratch_shapes=[pltpu.VMEM((tm, d), dt)],
        compiler_params=_cparams(3),
        name="ret_in_proj",
    )(x, mod.arr, mod.arr, g_mix, w_in, cos, sin)


def _log_gamma(h):
    return math.log(1.0 - 2.0 ** (-5.0 - h))


def _ret_prompt_kernel(q_ref, k_ref, v_ref, g_ref, o_ref, s_ref, *, chunk):
    @pl.when(pl.program_id(1) == 0)
    def _():
        s_ref[...] = jnp.zeros_like(s_ref)

    ti = lax.broadcasted_iota(jnp.int32, (chunk, chunk), 0)
    tj = lax.broadcasted_iota(jnp.int32, (chunk, chunk), 1)
    rel = (ti - tj).astype(F32)
    t = lax.broadcasted_iota(jnp.int32, (chunk, 1), 0).astype(F32)
    for h in range(RET_HEADS):
        lg = _log_gamma(h)
        decay = jnp.where(rel >= 0, jnp.exp(lg * jnp.maximum(rel, 0.0)), 0.0)
        qh = q_ref[:, h * RET_DK:(h + 1) * RET_DK]
        kh = k_ref[:, h * RET_DK:(h + 1) * RET_DK]
        vh = v_ref[:, h * RET_DV:(h + 1) * RET_DV]
        a = _dot_nt(qh, kh) * decay
        inner = _dot(a.astype(BF16), vh)
        s_old = s_ref[h]
        cross = _dot(qh, s_old.astype(BF16)) * jnp.exp((t + 1.0) * lg)
        o = inner + cross
        kd = (kh.astype(F32) * jnp.exp((chunk - 1.0 - t) * lg)).astype(BF16)
        s_ref[h] = math.exp(chunk * lg) * s_old + _dot_tn(kd, vh)
        o = o * lax.rsqrt(jnp.mean(o * o, axis=-1, keepdims=True) + EPS)
        gh = g_ref[:, h * RET_DV:(h + 1) * RET_DV].astype(F32)
        o_ref[:, h * RET_DV:(h + 1) * RET_DV] = (o * _silu(gh)).astype(BF16)


def _ret_prompt_call(proj, chunk):
    b, l, _ = proj.shape
    qd = RET_HEADS * RET_DK
    vd = RET_HEADS * RET_DV
    col = lambda w, j: pl.BlockSpec((None, chunk, w), lambda bb, c: (bb, c, j))
    return pl.pallas_call(
        functools.partial(_ret_prompt_kernel, chunk=chunk),
        grid=(b, l // chunk),
        in_specs=[col(qd, 0), col(qd, 1), col(vd, 1), col(vd, 2)],
        out_specs=[_tok_spec(chunk, vd),
                   pl.BlockSpec((None, RET_HEADS, RET_DK, RET_DV), lambda bb, c: (bb, 0, 0, 0))],
        out_shape=[jax.ShapeDtypeStruct((b, l, vd), BF16),
                   jax.ShapeDtypeStruct((b, RET_HEADS, RET_DK, RET_DV), F32)],
        compiler_params=_cparams(2),
        name="retention_prompt",
    )(proj, proj, proj, proj)


def _ret_sample_kernel(q_ref, k_ref, v_ref, g_ref, s_ref, o_ref, sn_ref):
    b = pl.program_id(0)
    h = pl.program_id(1)
    nb = q_ref.shape[0]
    gamma = 1.0 - 1.0 / (jnp.zeros((1, 1), F32) + (jnp.int32(32) << h).astype(F32))
    rowsel = lax.broadcasted_iota(jnp.int32, (nb, 1), 0) == b
    q = jnp.where(rowsel, q_ref[...], 0.0)
    k = jnp.where(rowsel, k_ref[...], 0.0)
    v = v_ref[...]
    vb = jnp.sum(jnp.where(rowsel, v, 0.0), axis=0, keepdims=True)
    gb = jnp.sum(jnp.where(rowsel, g_ref[...], 0.0), axis=0, keepdims=True)
    s_old = s_ref[...]
    kv = lax.dot_general(k, v, (((0,), (0,)), ((), ())), preferred_element_type=F32,
                         precision=lax.Precision.HIGHEST)
    sn_ref[...] = gamma * s_old + kv
    qk = jnp.sum(jnp.sum(q * k, axis=1, keepdims=True), axis=0, keepdims=True)
    cross = jnp.sum(_mm(q, s_old, True), axis=0, keepdims=True) * gamma
    o = qk * vb + cross
    o = o * lax.rsqrt(jnp.mean(o * o, axis=-1, keepdims=True) + EPS)
    o_ref[...] = o * _silu(gb)


def _ret_sample_call(proj, s0):
    nb = proj.shape[0]
    k0 = RET_HEADS
    v0 = 2 * RET_HEADS * RET_DK // RET_DV
    g0 = v0 + RET_HEADS
    return pl.pallas_call(
        _ret_sample_kernel,
        grid=(nb, RET_HEADS),
        in_specs=[
            pl.BlockSpec((nb, RET_DK), lambda b, h: (0, h)),
            pl.BlockSpec((nb, RET_DK), lambda b, h: (0, k0 + h)),
            pl.BlockSpec((nb, RET_DV), lambda b, h: (0, v0 + h)),
            pl.BlockSpec((nb, RET_DV), lambda b, h: (0, g0 + h)),
            pl.BlockSpec((None, None, RET_DK, RET_DV), lambda b, h: (b, h, 0, 0)),
        ],
        out_specs=[pl.BlockSpec((None, 1, RET_DV), lambda b, h: (b, 0, h)),
                   pl.BlockSpec((None, None, RET_DK, RET_DV), lambda b, h: (b, h, 0, 0))],
        out_shape=[jax.ShapeDtypeStruct((nb, 1, RET_HEADS * RET_DV), F32),
                   jax.ShapeDtypeStruct((nb, RET_HEADS, RET_DK, RET_DV), F32)],
        compiler_params=_cparams(2),
        name="retention_sample",
    )(proj, proj, proj, proj, s0)


def _route(logits):
    tm = logits.shape[0]
    lane = lax.broadcasted_iota(jnp.int32, (tm, LANES), 1).astype(F32)
    ninf = -jnp.inf
    lg = jnp.where(lane < N_GROUPS, logits[:, :LANES], ninf)
    mx = jnp.max(lg, axis=-1, keepdims=True)
    p_group = 1.0 / jnp.sum(jnp.exp(lg - mx), axis=-1, keepdims=True)
    gi = jnp.min(jnp.where(lg == mx, lane, float(LANES)), axis=-1, keepdims=True)
    lo = gi * EXPERTS_PER_GROUP
    les = jnp.where((lane >= lo) & (lane < lo + EXPERTS_PER_GROUP), logits[:, LANES:], ninf)
    v1 = jnp.max(les, axis=-1, keepdims=True)
    i1 = jnp.min(jnp.where(les == v1, lane, float(LANES)), axis=-1, keepdims=True)
    les2 = jnp.where(lane == i1, ninf, les)
    v2 = jnp.max(les2, axis=-1, keepdims=True)
    i2 = jnp.min(jnp.where(les2 == v2, lane, float(LANES)), axis=-1, keepdims=True)
    e2 = jnp.exp(v2 - v1)
    w1 = 1.0 / (1.0 + e2)
    w2 = e2 / (1.0 + e2)
    return jnp.where(lane == i1, w1 * p_group, 0.0) + jnp.where(lane == i2, w2 * p_group, 0.0)


def _mixout_kernel(o_ref, w_ref, x_ref, g1_ref, sh_ref, sc_ref, gf_ref, wr_ref, x1_ref, h2_ref, gates_ref, *,
                   precise):
    y = _mm(o_ref[...], w_ref[...], precise)
    x1 = x_ref[...] + g1_ref[...] * y
    x1_ref[...] = x1
    h2 = _rms_mod(x1, gf_ref[...], sh_ref[...], sc_ref[...])
    h2_ref[...] = h2.astype(h2_ref.dtype)
    gates_ref[...] = _route(_mm(h2, wr_ref[...], True))


def _mixout_call(o, w_out, x, mod, layer, g_ffn, w_router, tm, precise):
    bx, lx, d = x.shape
    kd = o.shape[-1]
    return pl.pallas_call(
        functools.partial(_mixout_kernel, precise=precise),
        grid=(bx, lx // tm),
        in_specs=[
            _tok_spec(tm, kd),
            _const_spec((kd, d)),
            _tok_spec(tm, d),
            mod.spec(layer, 2), mod.spec(layer, 3), mod.spec(layer, 4),
            _const_spec((1, d)),
            _const_spec((d, 2 * LANES)),
        ],
        out_specs=[_tok_spec(tm, d), _tok_spec(tm, d), _tok_spec(tm, LANES)],
        out_shape=[jax.ShapeDtypeStruct((bx, lx, d), F32),
                   jax.ShapeDtypeStruct((bx, lx, d), F32 if precise else BF16),
                   jax.ShapeDtypeStruct((bx, lx, LANES), F32)],
        compiler_params=_cparams(2),
        name="mix_out_router",
    )(o, w_out, x, mod.arr, mod.arr, mod.arr, g_ffn, w_router)


def _moe_kernel(h_ref, gates_ref, wg_ref, wu_ref, wd_ref, x1_ref, g2_ref, gfin_ref, out_ref, acc_ref, *, final,
                precise):
    e = pl.program_id(2)

    @pl.when(e == 0)
    def _():
        acc_ref[...] = jnp.zeros_like(acc_ref)

    h = h_ref[...]
    a = _mm(h, wg_ref[...], precise)
    u = _mm(h, wu_ref[...], precise)
    lane = lax.broadcasted_iota(jnp.int32, gates_ref.shape, 1)
    ge = jnp.sum(jnp.where(lane == e, gates_ref[...], 0.0), axis=-1, keepdims=True)
    act = (_silu(a) * u) * ge
    acc_ref[...] += _mm(act, wd_ref[...], precise)

    @pl.when(e == pl.num_programs(2) - 1)
    def _():
        x2 = x1_ref[...] + g2_ref[...] * acc_ref[...]
        if final:
            x2 = (x2 * lax.rsqrt(jnp.mean(x2 * x2, axis=-1, keepdims=True) + EPS)) * gfin_ref[...]
        out_ref[...] = x2


def _moe_call(h2, gates, wg, wu, wd, layer, x1, mod, g_final, tm, final, precise):
    bx, lx, d = x1.shape
    tok3 = lambda w: pl.BlockSpec((None, tm, w), lambda b, i, e: (b, i, 0))
    g2_spec2 = mod.spec(layer, 5)
    g2_spec = pl.BlockSpec(g2_spec2.block_shape, lambda b, i, e: g2_spec2.index_map(b, i))
    return pl.pallas_call(
        functools.partial(_moe_kernel, final=final, precise=precise),
        grid=(bx, lx // tm, N_EXPERTS),
        in_specs=[
            tok3(d), tok3(LANES),
            pl.BlockSpec((None, d, D_EXPERT), lambda b, i, e: (layer, 0, e)),
            pl.BlockSpec((None, d, D_EXPERT), lambda b, i, e: (layer, 0, e)),
            pl.BlockSpec((None, D_EXPERT, d), lambda b, i, e: (layer, e, 0)),
            tok3(d),
            g2_spec,
            pl.BlockSpec((1, d), lambda b, i, e: (0, 0)),
        ],
        out_specs=tok3(d),
        out_shape=jax.ShapeDtypeStruct((bx, lx, d), F32),
        scratch_shapes=[pltpu.VMEM((tm, d), F32)],
        compiler_params=_cparams(3),
        name="hier_moe",
    )(h2, gates, wg, wu, wd, x1, mod.arr, g_final)


def _log_sigmoid(z):
    return -(jnp.maximum(-z, 0.0) + jnp.log1p(jnp.exp(-jnp.abs(z))))


def _kvq_kernel(x_ref, shk_ref, sck_ref, gkv_ref, wkv_ref, wf_ref, bf_ref, sh1_ref, sc1_ref, gmix_ref, wq_ref,
                k32_ref, v32_ref, k16_ref, v16_ref, lf_ref, q_ref, *, precise):
    x = x_ref[...]
    fd = FOX_HEADS * FOX_HD
    n = _rms_mod(x, gkv_ref[...], shk_ref[...], sck_ref[...])
    if not precise:
        n = n.astype(BF16)
    k = _mm(n, wkv_ref[:, 0:fd], precise)
    k32_ref[...] = k
    k16_ref[...] = k.astype(BF16)
    v = _mm(n, wkv_ref[:, fd:2 * fd], precise)
    v32_ref[...] = v
    v16_ref[...] = v.astype(BF16)
    z = _mm(n, wf_ref[...], precise) + bf_ref[...]
    lf_ref[...] = _log_sigmoid(z)[:, :FOX_HEADS]
    h = _rms_mod(x, gmix_ref[...], sh1_ref[...], sc1_ref[...])
    q_ref[...] = (_mm(h, wq_ref[...], precise) * (FOX_HD ** -0.5)).astype(q_ref.dtype)


def _kvq_call(x, kvmod, mod, g_kv, w_kv, w_f, b_f, g_mix, w_q, tm, precise):
    bx, lx, d = x.shape
    fd = FOX_HEADS * FOX_HD
    return pl.pallas_call(
        functools.partial(_kvq_kernel, precise=precise),
        grid=(bx, lx // tm),
        in_specs=[
            _tok_spec(tm, d),
            kvmod.spec(0, 0), kvmod.spec(0, 1),
            _const_spec((1, d)),
            _const_spec((d, 2 * fd)),
            _const_spec((d, LANES)),
            _const_spec((1, LANES)),
            mod.spec(1, 0), mod.spec(1, 1),
            _const_spec((1, d)),
            _const_spec((d, fd)),
        ],
        out_specs=[_tok_spec(tm, fd), _tok_spec(tm, fd), _tok_spec(tm, fd), _tok_spec(tm, fd),
                   _tok_spec(tm, FOX_HEADS), _tok_spec(tm, fd)],
        out_shape=[jax.ShapeDtypeStruct((bx, lx, fd), F32), jax.ShapeDtypeStruct((bx, lx, fd), F32),
                   jax.ShapeDtypeStruct((bx, lx, fd), BF16), jax.ShapeDtypeStruct((bx, lx, fd), BF16),
                   jax.ShapeDtypeStruct((bx, lx, FOX_HEADS), F32),
                   jax.ShapeDtypeStruct((bx, lx, fd), F32 if precise else BF16)],
        compiler_params=_cparams(2),
        name="kv_q_proj",
    )(x, kvmod.arr, kvmod.arr, g_kv, w_kv, w_f, b_f, mod.arr, mod.arr, g_mix, w_q)


def _cumsum_kernel(x_ref, o_ref):
    nh, l = x_ref.shape
    r = lax.broadcasted_iota(jnp.int32, (LANES, LANES), 0)
    c = lax.broadcasted_iota(jnp.int32, (LANES, LANES), 1)
    upper = (r <= c).astype(BF16)
    carry = jnp.zeros((nh, 1), F32)
    for blk in range(l // LANES):
        hi, mid, lo = _split3(x_ref[:, blk * LANES:(blk + 1) * LANES])
        cs = (_dot(hi, upper) + _dot(mid, upper)) + _dot(lo, upper) + carry
        o_ref[:, blk * LANES:(blk + 1) * LANES] = cs
        carry = cs[:, LANES - 1:LANES]


def _cumsum_call(x):
    b, nh, l = x.shape
    return pl.pallas_call(
        _cumsum_kernel,
        grid=(b,),
        in_specs=[pl.BlockSpec((None, nh, l), lambda i: (i, 0, 0))],
        out_specs=pl.BlockSpec((None, nh, l), lambda i: (i, 0, 0)),
        out_shape=jax.ShapeDtypeStruct((b, nh, l), F32),
        compiler_params=_cparams(1),
        name="logf_cumsum",
    )(x)


def _fox_kernel(q_ref, k_ref, v_ref, fq_ref, fk_ref, o_ref, acc_ref, *, t):
    qi = pl.program_id(2)
    first = lax.broadcasted_iota(jnp.int32, (1, LANES), 1) < FOX_HD
    col_a = lax.broadcasted_iota(jnp.int32, (1, 2 * t), 1) < t
    q2 = q_ref[...]
    zero = jnp.zeros_like(q2)
    qs = jnp.concatenate([jnp.where(first, q2, zero), jnp.where(first, zero, q2)], axis=0)
    fq = jnp.concatenate([fq_ref[0:1, :], fq_ref[1:2, :]], axis=1)
    key = lax.broadcasted_iota(jnp.int32, (t, 2 * t), 0)
    qry = lax.broadcasted_iota(jnp.int32, (t, 2 * t), 1)
    causal = key <= jnp.where(col_a, qry, qry - t)
    acc_ref[...] = jnp.zeros_like(acc_ref)

    def tile(j, m, l, diagonal):
        start = pl.multiple_of(j * t, t)
        kt = k_ref[pl.ds(start, t), :]
        vt = v_ref[pl.ds(start, t), :]
        fk = fk_ref[pl.ds(start, t), :]
        s = (_dot_nt(kt, qs) + fq) - jnp.where(col_a, fk[:, 0:1], fk[:, 1:2])
        if diagonal:
            s = jnp.where(causal, s, NEG_INF)
        m_new = jnp.maximum(m, jnp.max(s, axis=0, keepdims=True))
        p = jnp.exp(s - m_new)
        alpha = jnp.exp(m - m_new)
        acc_ref[...] = alpha * acc_ref[...] + _dot_tn(vt, p.astype(BF16))
        return m_new, alpha * l + jnp.sum(p, axis=0, keepdims=True)

    m0 = jnp.full((1, 2 * t), -jnp.inf, F32)
    l0 = jnp.zeros((1, 2 * t), F32)
    m, l = lax.fori_loop(0, qi, lambda j, c: tile(j, c[0], c[1], False), (m0, l0))
    m, l = tile(qi, m, l, True)
    o_t = acc_ref[...] / l
    row_a = lax.broadcasted_iota(jnp.int32, (LANES, 1), 0) < FOX_HD
    o_ref[...] = jnp.where(row_a, o_t[:, :t], o_t[:, t:]).T.astype(BF16)


def _fox_prompt_call(q, k16, v16, f_rows, f_cols, tq):
    b, l, fd = q.shape
    npair = fd // LANES
    return pl.pallas_call(
        functools.partial(_fox_kernel, t=tq),
        grid=(b, npair, l // tq),
        in_specs=[
            pl.BlockSpec((None, tq, LANES), lambda bb, hp, i: (bb, i, hp)),
            pl.BlockSpec((None, l, LANES), lambda bb, hp, i: (bb, 0, hp)),
            pl.BlockSpec((None, l, LANES), lambda bb, hp, i: (bb, 0, hp)),
            pl.BlockSpec((None, None, None, 2, tq), lambda bb, hp, i: (bb, hp, i, 0, 0)),
            pl.BlockSpec((None, None, l, 2), lambda bb, hp, i: (bb, hp, 0, 0)),
        ],
        out_specs=pl.BlockSpec((None, tq, LANES), lambda bb, hp, i: (bb, i, hp)),
        out_shape=jax.ShapeDtypeStruct((b, l, fd), BF16),
        scratch_shapes=[pltpu.VMEM((LANES, 2 * tq), F32)],
        compiler_params=_cparams(3),
        name="fox_attention_prompt",
    )(q, k16, v16, f_rows, f_cols)


def _attn_sample_kernel(pt_ref, q_ref, kn_ref, vn_ref, lfn_ref, *refs, pps, page):
    ck = refs[0:pps]
    cv = refs[pps:2 * pps]
    clf = refs[2 * pps:3 * pps]
    o_ref = refs[3 * pps]
    m_ref, l_ref, acc_ref, carry_ref = refs[3 * pps + 1:]
    step = pl.program_id(1)
    nh, hd = FOX_HEADS, FOX_HD
    rows = page * nh
    q = q_ref[...]

    @pl.when(step == 0)
    def _():
        m_ref[...] = jnp.sum(q * kn_ref[...], axis=-1, keepdims=True)
        l_ref[...] = jnp.ones_like(l_ref)
        acc_ref[...] = vn_ref[...]
        carry_ref[...] = lfn_ref[...]

    valid = (lax.broadcasted_iota(jnp.int32, (nh, rows), 1) % nh) == lax.broadcasted_iota(jnp.int32, (nh, rows), 0)
    tt = lax.broadcasted_iota(jnp.int32, (page, rows + LANES), 0)
    cc = lax.broadcasted_iota(jnp.int32, (page, rows + LANES), 1)
    later = ((tt > cc // nh) | (cc >= rows)).astype(BF16)

    def split2(x):
        hi = x.astype(BF16)
        return jnp.concatenate([hi, (x - hi.astype(F32)).astype(BF16)], axis=0)

    q2 = split2(q)
    lf_parts = [part for i in range(pps) for part in _split3(clf[i][...])]
    lfx_all = _dot(jnp.concatenate(lf_parts, axis=0), later)
    for i in range(pps):
        base = 3 * nh * i
        lfx = (lfx_all[base:base + nh] + lfx_all[base + nh:base + 2 * nh]) + lfx_all[base + 2 * nh:base + 3 * nh]
        carry = carry_ref[...]
        bias = lfx[:, :rows] + carry
        carry_ref[...] = carry + lfx[:, rows:rows + 1]
        z2 = _dot_nt(q2, ck[i][...].reshape(rows, hd).astype(BF16))
        z = jnp.where(valid, (z2[:nh] + z2[nh:]) + bias, -jnp.inf)
        m_old = m_ref[...]
        m_new = jnp.maximum(m_old, jnp.max(z, axis=-1, keepdims=True))
        p = jnp.exp(z - m_new)
        alpha = jnp.exp(m_old - m_new)
        m_ref[...] = m_new
        l_ref[...] = alpha * l_ref[...] + jnp.sum(p, axis=-1, keepdims=True)
        pv = _dot(split2(p), cv[i][...].reshape(rows, hd).astype(BF16))
        acc_ref[...] = alpha * acc_ref[...] + (pv[:nh] + pv[nh:])

    @pl.when(step == pl.num_programs(1) - 1)
    def _():
        o_ref[...] = acc_ref[...] / l_ref[...]


def _attn_sample_call(page_table, q, k_new, v_new, lf_new, cache_k, cache_v, cache_lf, pps):
    nb, n_pages = page_table.shape
    n_pool, page, nh, hd = cache_k.shape
    steps = n_pages // pps

    def page_map(i, nd):
        return lambda b, s, pt: (pt[b, n_pages - 1 - (s * pps + i)],) + (0,) * nd

    row = lambda w: pl.BlockSpec((None, nh, w), lambda b, s, pt: (b, 0, 0))
    in_specs = [row(hd), row(hd), row(hd), row(1)]
    in_specs += [pl.BlockSpec((None, page, nh, hd), page_map(i, 3)) for i in range(pps)]
    in_specs += [pl.BlockSpec((None, page, nh, hd), page_map(i, 3)) for i in range(pps)]
    in_specs += [pl.BlockSpec((None, nh, page), page_map(i, 2)) for i in range(pps)]
    grid_spec = pltpu.PrefetchScalarGridSpec(
        num_scalar_prefetch=1,
        grid=(nb, steps),
        in_specs=in_specs,
        out_specs=row(hd),
        scratch_shapes=[pltpu.VMEM((nh, 1), F32), pltpu.VMEM((nh, 1), F32), pltpu.VMEM((nh, hd), F32),
                        pltpu.VMEM((nh, 1), F32)],
    )
    return pl.pallas_call(
        functools.partial(_attn_sample_kernel, pps=pps, page=page),
        grid_spec=grid_spec,
        out_shape=jax.ShapeDtypeStruct((nb, nh, hd), F32),
        compiler_params=_cparams(2),
        name="fox_attention_sample",
    )(page_table, q, k_new, v_new, lf_new, *([cache_k] * pps), *([cache_v] * pps), *([cache_lf] * pps))


def _rope_tables(pos):
    half = RET_DK // 2
    inv = ROPE_BASE ** (-jnp.arange(half, dtype=F32) / half)
    ang = pos.astype(F32)[:, None] * inv[None, :]
    return jnp.cos(ang), jnp.sin(ang)


def _pick_tile(n, pref):
    t = min(n, pref)
    while n % t:
        t //= 2
    return t


def kernel(x_prompt, x_sample, c_prompt, c_sample, state_ret, cache_k, cache_v, cache_logf, page_table, w_mod, b_mod, g_mix, g_ffn, w_ret_in, w_ret_out, g_kv, w_kvmod, b_kvmod, w_kv, w_f, b_f, w_fq, w_fo, w_rg, w_re, w_e_gate, w_e_up, w_e_down, g_final):
    b, l, d = x_prompt.shape
    nb = x_sample.shape[0]
    n_pool, page = cache_k.shape[0], cache_k.shape[1]
    n_pages = page_table.shape[1]
    past_len = n_pages * page
    fd = FOX_HEADS * FOX_HD

    c_all = jnp.concatenate([c_sample, c_prompt], axis=0)
    mod_all = _mod_call(c_all, w_mod, b_mod, 1536)
    kvmod_all = _mod_call(c_all, w_kvmod[None], b_kvmod[None], 1024)

    w_f_pad = jnp.pad(w_f, ((0, 0), (0, LANES - FOX_HEADS)))
    weights32 = dict(w_in=w_ret_in, w_out=w_ret_out[0], w_kv=w_kv, w_fq=w_fq[0], w_fo=w_fo[0], w_f=w_f_pad,
                     wg=w_e_gate, wu=w_e_up, wd=w_e_down)
    weights16 = {name: w.astype(BF16) for name, w in weights32.items()}
    b_f_row = jnp.pad(b_f, (0, LANES - FOX_HEADS)).reshape(1, LANES)
    w_router = jnp.concatenate([jnp.pad(w_rg, ((0, 0), (0, 0), (0, LANES - N_GROUPS))),
                                jnp.pad(w_re, ((0, 0), (0, 0), (0, LANES - N_EXPERTS)))], axis=-1)
    g_mix2 = g_mix.reshape(-1, 1, d)
    g_ffn2 = g_ffn.reshape(-1, 1, d)
    g_kv2 = g_kv.reshape(1, d)
    g_fin2 = g_final.reshape(1, d)

    def trunk(x, mod, kvmod, pos, tm, tm_in, tm_moe, mixer0, mixer1, precise):
        w = weights32 if precise else weights16
        cos, sin = _rope_tables(pos)
        proj = _retin_call(x, mod, g_mix2[0], w["w_in"], cos, sin, tm_in, precise)
        o, s_new = mixer0(proj)
        x1, h2, gates = _mixout_call(o, w["w_out"], x, mod, 0, g_ffn2[0], w_router[0], tm, precise)
        x2 = _moe_call(h2, gates, w["wg"], w["wu"], w["wd"], 0, x1, mod, g_fin2, tm_moe, False, precise)
        k32, v32, k16, v16, lf, qf = _kvq_call(x2, kvmod, mod, g_kv2, w["w_kv"], w["w_f"], b_f_row, g_mix2[1],
                                               w["w_fq"], tm, precise)
        o1 = mixer1(qf, k32, v32, k16, v16, lf)
        x3, h4, gates1 = _mixout_call(o1, w["w_fo"], x2, mod, 1, g_ffn2[1], w_router[1], tm, precise)
        y = _moe_call(h4, gates1, w["wg"], w["wu"], w["wd"], 1, x3, mod, g_fin2, tm_moe, True, precise)
        return y, s_new, k32, v32, lf

    tm_p = _pick_tile(l, 512)
    chunk = _pick_tile(l, 256)
    tq = _pick_tile(l, 512)

    def ret_prompt(proj):
        return _ret_prompt_call(proj, chunk)

    def fox_prompt(qf, k32, v32, k16, v16, lf):
        f_t = _cumsum_call(jnp.swapaxes(lf, 1, 2))
        f_pairs = f_t.reshape(b, FOX_HEADS // 2, 2, l)
        f_cols = jnp.swapaxes(f_pairs, 2, 3)
        f_rows = jnp.swapaxes(f_pairs.reshape(b, FOX_HEADS // 2, 2, l // tq, tq), 2, 3)
        return _fox_prompt_call(qf, k16, v16, f_rows, f_cols, tq)

    mod_p = _Mod(mod_all, False, nb, tm_p)
    kvmod_p = _Mod(kvmod_all, False, nb, tm_p)
    y_p, s_p, k_p, v_p, lf_p = trunk(x_prompt, mod_p, kvmod_p, jnp.arange(l), tm_p, _pick_tile(l, 1024),
                                     _pick_tile(l, 1024), ret_prompt, fox_prompt, False)

    def ret_sample(proj):
        o, s_new = _ret_sample_call(proj[0], state_ret[0])
        return o.reshape(1, nb, -1), s_new

    def fox_sample(qf, k32, v32, k16, v16, lf):
        heads = lambda a: a.reshape(nb, FOX_HEADS, FOX_HD)
        o = _attn_sample_call(page_table, heads(qf), heads(k32), heads(v32), lf.reshape(nb, FOX_HEADS, 1),
                              cache_k, cache_v, jnp.swapaxes(cache_logf, 1, 2), _pick_tile(n_pages, 4))
        return o.reshape(1, nb, fd)

    mod_s = _Mod(mod_all, True, 0, nb)
    kvmod_s = _Mod(kvmod_all, True, 0, nb)
    y_s, s_s, k_s, v_s, lf_s = trunk(x_sample.reshape(1, nb, d), mod_s, kvmod_s,
                                     jnp.full((nb,), past_len, jnp.int32), nb, nb, nb, ret_sample, fox_sample, True)

    return (y_p, y_s.reshape(nb, 1, d), s_p[None], s_s[None],
            k_p.reshape(b, l, FOX_HEADS, FOX_HD), v_p.reshape(b, l, FOX_HEADS, FOX_HD), lf_p,
            k_s.reshape(nb, 1, FOX_HEADS, FOX_HD), v_s.reshape(nb, 1, FOX_HEADS, FOX_HD),
            lf_s.reshape(nb, 1, FOX_HEADS))
```

```python
import functools
import math

import jax
import jax.numpy as jnp
from jax import lax
from jax.experimental import pallas as pl
from jax.experimental.pallas import tpu as pltpu

D_MODEL = 1024
RET_HEADS = 4
RET_DK = 256
RET_DV = 512
ROPE_BASE = 10000.0
FOX_HEADS = 16
FOX_HD = 64
N_GROUPS = 4
EXPERTS_PER_GROUP = 8
N_EXPERTS = 32
D_EXPERT = 256
EPS = 1e-6
NEG_INF = -1e30

LANES = 128
VMEM_LIMIT = 56 * 1024 * 1024
F32 = jnp.float32
BF16 = jnp.bfloat16


def _cparams(n_axes):
    return pltpu.CompilerParams(dimension_semantics=("arbitrary",) * n_axes,
                                vmem_limit_bytes=VMEM_LIMIT)


def _silu(x):
    return x * jax.nn.sigmoid(x)


def _dot(a, b):
    return jnp.dot(a, b, preferred_element_type=F32)


def _dot_nt(a, b):
    return lax.dot_general(a, b, (((1,), (1,)), ((), ())), preferred_element_type=F32)


def _dot_tn(a, b):
    return lax.dot_general(a, b, (((0,), (0,)), ((), ())), preferred_element_type=F32)


def _mm(a, w, precise):
    if precise:
        return jnp.dot(a, w, preferred_element_type=F32, precision=lax.Precision.HIGHEST)
    return _dot(a.astype(BF16), w)


def _split3(x):
    hi = x.astype(BF16)
    r1 = x - hi.astype(F32)
    mid = r1.astype(BF16)
    lo = (r1 - mid.astype(F32)).astype(BF16)
    return hi, mid, lo


def _rms_mod(x, g, shift, scale):
    y = x * lax.rsqrt(jnp.mean(x * x, axis=-1, keepdims=True) + EPS)
    return (y * g) * (1.0 + scale) + shift


def _mod_kernel(c_ref, w_ref, b_ref, o_ref):
    o_ref[...] = _mm(_silu(c_ref[...]), w_ref[...], True) + b_ref[...]


def _mod_call(c, w, b, tn):
    ns, k, n = w.shape
    m = c.shape[0]
    return pl.pallas_call(
        _mod_kernel,
        grid=(ns, n // tn),
        in_specs=[
            pl.BlockSpec((m, k), lambda s, j: (0, 0)),
            pl.BlockSpec((None, k, tn), lambda s, j: (s, 0, j)),
            pl.BlockSpec((None, 1, tn), lambda s, j: (s, 0, j)),
        ],
        out_specs=pl.BlockSpec((None, m, tn), lambda s, j: (s, 0, j)),
        out_shape=jax.ShapeDtypeStruct((ns, m, n), F32),
        compiler_params=_cparams(2),
        name="adaln_mod",
    )(c, w, b.reshape(ns, 1, n))


class _Mod:
    def __init__(self, arr, per_token, row0, tm):
        self.per_token = per_token
        self.row0 = row0
        self.tm = tm
        self.arr = arr if per_token else arr.reshape(arr.shape[0], arr.shape[1], 1, arr.shape[2])

    def spec(self, stack, col):
        if self.per_token:
            assert self.row0 % self.tm == 0
            r0 = self.row0 // self.tm
            return pl.BlockSpec((None, self.tm, D_MODEL), lambda b, i: (stack, r0 + i, col))
        row0 = self.row0
        return pl.BlockSpec((None, None, 1, D_MODEL), lambda b, i: (stack, row0 + b, 0, col))


def _tok_spec(tm, width):
    return pl.BlockSpec((None, tm, width), lambda b, i: (b, i, 0))


def _const_spec(shape):
    nd = len(shape)
    return pl.BlockSpec(shape, lambda b, i: (0,) * nd)


def _retin_kernel(x_ref, sh_ref, sc_ref, g_ref, w_ref, cos_ref, sin_ref, o_ref, h_ref, *, precise):
    j = pl.program_id(2)

    @pl.when(j == 0)
    def _():
        h_ref[...] = _rms_mod(x_ref[...], g_ref[...], sh_ref[...], sc_ref[...]).astype(h_ref.dtype)

    p = _mm(h_ref[...], w_ref[...], precise)

    @pl.when(j < 2)
    def _():
        scale = jnp.where(j == 0, 1.0, RET_DK ** -0.5)
        half = RET_DK // 2
        cos = cos_ref[...]
        sin = sin_ref[...]
        for hh in range(RET_HEADS):
            a = hh * RET_DK
            x1 = p[:, a:a + half]
            x2 = p[:, a + half:a + RET_DK]
            o_ref[:, a:a + half] = ((x1 * cos - x2 * sin) * scale).astype(o_ref.dtype)
            o_ref[:, a + half:a + RET_DK] = ((x1 * sin + x2 * cos) * scale).astype(o_ref.dtype)

    @pl.when(j >= 2)
    def _():
        o_ref[...] = p.astype(o_ref.dtype)


def _retin_call(x, mod, g_mix, w_in, cos, sin, tm, precise):
    bx, lx, d = x.shape
    tn = RET_HEADS * RET_DK
    n = w_in.shape[-1]
    dt = F32 if precise else BF16
    spec3 = lambda sp: pl.BlockSpec(sp.block_shape, lambda b, i, j: sp.index_map(b, i))
    return pl.pallas_call(
        functools.partial(_retin_kernel, precise=precise),
        grid=(bx, lx // tm, n // tn),
        in_specs=[
            spec3(_tok_spec(tm, d)),
            spec3(mod.spec(0, 0)), spec3(mod.spec(0, 1)),
            spec3(_const_spec((1, d))),
            pl.BlockSpec((None, d, tn), lambda b, i, j: (0, 0, j)),
            pl.BlockSpec((tm, RET_DK // 2), lambda b, i, j: (i, 0)),
            pl.BlockSpec((tm, RET_DK // 2), lambda b, i, j: (i, 0)),
        ],
        out_specs=pl.BlockSpec((None, tm, tn), lambda b, i, j: (b, i, j)),
        out_shape=jax.ShapeDtypeStruct((bx, lx, n), dt),
        scratch_shapes=[pltpu.VMEM((tm, d), dt)],
        compiler_params=_cparams(3),
        name="ret_in_proj",
    )(x, mod.arr, mod.arr, g_mix, w_in, cos, sin)


def _log_gamma(h):
    return math.log(1.0 - 2.0 ** (-5.0 - h))


def _ret_prompt_kernel(q_ref, k_ref, v_ref, g_ref, o_ref, s_ref, *, chunk):
    @pl.when(pl.program_id(1) == 0)
    def _():
        s_ref[...] = jnp.zeros_like(s_ref)

    ti = lax.broadcasted_iota(jnp.int32, (chunk, chunk), 0)
    tj = lax.broadcasted_iota(jnp.int32, (chunk, chunk), 1)
    rel = (ti - tj).astype(F32)
    t = lax.broadcasted_iota(jnp.int32, (chunk, 1), 0).astype(F32)
    for h in range(RET_HEADS):
        lg = _log_gamma(h)
        decay = jnp.where(rel >= 0, jnp.exp(lg * jnp.maximum(rel, 0.0)), 0.0)
        qh = q_ref[:, h * RET_DK:(h + 1) * RET_DK]
        kh = k_ref[:, h * RET_DK:(h + 1) * RET_DK]
        vh = v_ref[:, h * RET_DV:(h + 1) * RET_DV]
        a = _dot_nt(qh, kh) * decay
        inner = _dot(a.astype(BF16), vh)
        s_old = s_ref[h]
        cross = _dot(qh, s_old.astype(BF16)) * jnp.exp((t + 1.0) * lg)
        o = inner + cross
        kd = (kh.astype(F32) * jnp.exp((chunk - 1.0 - t) * lg)).astype(BF16)
        s_ref[h] = math.exp(chunk * lg) * s_old + _dot_tn(kd, vh)
        o = o * lax.rsqrt(jnp.mean(o * o, axis=-1, keepdims=True) + EPS)
        gh = g_ref[:, h * RET_DV:(h + 1) * RET_DV].astype(F32)
        o_ref[:, h * RET_DV:(h + 1) * RET_DV] = (o * _silu(gh)).astype(BF16)


def _ret_prompt_call(proj, chunk):
    b, l, _ = proj.shape
    qd = RET_HEADS * RET_DK
    vd = RET_HEADS * RET_DV
    col = lambda w, j: pl.BlockSpec((None, chunk, w), lambda bb, c: (bb, c, j))
    return pl.pallas_call(
        functools.partial(_ret_prompt_kernel, chunk=chunk),
        grid=(b, l // chunk),
        in_specs=[col(qd, 0), col(qd, 1), col(vd, 1), col(vd, 2)],
        out_specs=[_tok_spec(chunk, vd),
                   pl.BlockSpec((None, RET_HEADS, RET_DK, RET_DV), lambda bb, c: (bb, 0, 0, 0))],
        out_shape=[jax.ShapeDtypeStruct((b, l, vd), BF16),
                   jax.ShapeDtypeStruct((b, RET_HEADS, RET_DK, RET_DV), F32)],
        compiler_params=_cparams(2),
        name="retention_prompt",
    )(proj, proj, proj, proj)


def _ret_sample_kernel(q_ref, k_ref, v_ref, g_ref, s_ref, o_ref, sn_ref):
    b = pl.program_id(0)
    h = pl.program_id(1)
    nb = q_ref.shape[0]
    gamma = 1.0 - 1.0 / (jnp.zeros((1, 1), F32) + (jnp.int32(32) << h).astype(F32))
    rowsel = lax.broadcasted_iota(jnp.int32, (nb, 1), 0) == b
    q = jnp.where(rowsel, q_ref[...], 0.0)
    k = jnp.where(rowsel, k_ref[...], 0.0)
    v = v_ref[...]
    vb = jnp.sum(jnp.where(rowsel, v, 0.0), axis=0, keepdims=True)
    gb = jnp.sum(jnp.where(rowsel, g_ref[...], 0.0), axis=0, keepdims=True)
    s_old = s_ref[...]
    kv = lax.dot_general(k, v, (((0,), (0,)), ((), ())), preferred_element_type=F32,
                         precision=lax.Precision.HIGHEST)
    sn_ref[...] = gamma * s_old + kv
    qk = jnp.sum(jnp.sum(q * k, axis=1, keepdims=True), axis=0, keepdims=True)
    cross = jnp.sum(_mm(q, s_old, True), axis=0, keepdims=True) * gamma
    o = qk * vb + cross
    o = o * lax.rsqrt(jnp.mean(o * o, axis=-1, keepdims=True) + EPS)
    o_ref[...] = o * _silu(gb)


def _ret_sample_call(proj, s0):
    nb = proj.shape[0]
    k0 = RET_HEADS
    v0 = 2 * RET_HEADS * RET_DK // RET_DV
    g0 = v0 + RET_HEADS
    return pl.pallas_call(
        _ret_sample_kernel,
        grid=(nb, RET_HEADS),
        in_specs=[
            pl.BlockSpec((nb, RET_DK), lambda b, h: (0, h)),
            pl.BlockSpec((nb, RET_DK), lambda b, h: (0, k0 + h)),
            pl.BlockSpec((nb, RET_DV), lambda b, h: (0, v0 + h)),
            pl.BlockSpec((nb, RET_DV), lambda b, h: (0, g0 + h)),
            pl.BlockSpec((None, None, RET_DK, RET_DV), lambda b, h: (b, h, 0, 0)),
        ],
        out_specs=[pl.BlockSpec((None, 1, RET_DV), lambda b, h: (b, 0, h)),
                   pl.BlockSpec((None, None, RET_DK, RET_DV), lambda b, h: (b, h, 0, 0))],
        out_shape=[jax.ShapeDtypeStruct((nb, 1, RET_HEADS * RET_DV), F32),
                   jax.ShapeDtypeStruct((nb, RET_HEADS, RET_DK, RET_DV), F32)],
        compiler_params=_cparams(2),
        name="retention_sample",
    )(proj, proj, proj, proj, s0)


def _route(logits):
    tm = logits.shape[0]
    lane = lax.broadcasted_iota(jnp.int32, (tm, LANES), 1).astype(F32)
    ninf = -jnp.inf
    lg = jnp.where(lane < N_GROUPS, logits[:, :LANES], ninf)
    mx = jnp.max(lg, axis=-1, keepdims=True)
    p_group = 1.0 / jnp.sum(jnp.exp(lg - mx), axis=-1, keepdims=True)
    gi = jnp.min(jnp.where(lg == mx, lane, float(LANES)), axis=-1, keepdims=True)
    lo = gi * EXPERTS_PER_GROUP
    les = jnp.where((lane >= lo) & (lane < lo + EXPERTS_PER_GROUP), logits[:, LANES:], ninf)
    v1 = jnp.max(les, axis=-1, keepdims=True)
    i1 = jnp.min(jnp.where(les == v1, lane, float(LANES)), axis=-1, keepdims=True)
    les2 = jnp.where(lane == i1, ninf, les)
    v2 = jnp.max(les2, axis=-1, keepdims=True)
    i2 = jnp.min(jnp.where(les2 == v2, lane, float(LANES)), axis=-1, keepdims=True)
    e2 = jnp.exp(v2 - v1)
    w1 = 1.0 / (1.0 + e2)
    w2 = e2 / (1.0 + e2)
    return jnp.where(lane == i1, w1 * p_group, 0.0) + jnp.where(lane == i2, w2 * p_group, 0.0)


def _mixout_kernel(o_ref, w_ref, x_ref, g1_ref, sh_ref, sc_ref, gf_ref, wr_ref, x1_ref, h2_ref, gates_ref, *,
                   precise):
    y = _mm(o_ref[...], w_ref[...], precise)
    x1 = x_ref[...] + g1_ref[...] * y
    x1_ref[...] = x1
    h2 = _rms_mod(x1, gf_ref[...], sh_ref[...], sc_ref[...])
    h2_ref[...] = h2.astype(h2_ref.dtype)
    gates_ref[...] = _route(_mm(h2, wr_ref[...], True))


def _mixout_call(o, w_out, x, mod, layer, g_ffn, w_router, tm, precise):
    bx, lx, d = x.shape
    kd = o.shape[-1]
    return pl.pallas_call(
        functools.partial(_mixout_kernel, precise=precise),
        grid=(bx, lx // tm),
        in_specs=[
            _tok_spec(tm, kd),
            _const_spec((kd, d)),
            _tok_spec(tm, d),
            mod.spec(layer, 2), mod.spec(layer, 3), mod.spec(layer, 4),
            _const_spec((1, d)),
            _const_spec((d, 2 * LANES)),
        ],
        out_specs=[_tok_spec(tm, d), _tok_spec(tm, d), _tok_spec(tm, LANES)],
        out_shape=[jax.ShapeDtypeStruct((bx, lx, d), F32),
                   jax.ShapeDtypeStruct((bx, lx, d), F32 if precise else BF16),
                   jax.ShapeDtypeStruct((bx, lx, LANES), F32)],
        compiler_params=_cparams(2),
        name="mix_out_router",
    )(o, w_out, x, mod.arr, mod.arr, mod.arr, g_ffn, w_router)


def _moe_kernel(h_ref, gates_ref, wg_ref, wu_ref, wd_ref, x1_ref, g2_ref, gfin_ref, out_ref, acc_ref, *, final,
                precise):
    e = pl.program_id(2)

    @pl.when(e == 0)
    def _():
        acc_ref[...] = jnp.zeros_like(acc_ref)

    h = h_ref[...]
    a = _mm(h, wg_ref[...], precise)
    u = _mm(h, wu_ref[...], precise)
    lane = lax.broadcasted_iota(jnp.int32, gates_ref.shape, 1)
    ge = jnp.sum(jnp.where(lane == e, gates_ref[...], 0.0), axis=-1, keepdims=True)
    act = (_silu(a) * u) * ge
    acc_ref[...] += _mm(act, wd_ref[...], precise)

    @pl.when(e == pl.num_programs(2) - 1)
    def _():
        x2 = x1_ref[...] + g2_ref[...] * acc_ref[...]
        if final:
            x2 = (x2 * lax.rsqrt(jnp.mean(x2 * x2, axis=-1, keepdims=True) + EPS)) * gfin_ref[...]
        out_ref[...] = x2


def _moe_call(h2, gates, wg, wu, wd, layer, x1, mod, g_final, tm, final, precise):
    bx, lx, d = x1.shape
    tok3 = lambda w: pl.BlockSpec((None, tm, w), lambda b, i, e: (b, i, 0))
    g2_spec2 = mod.spec(layer, 5)
    g2_spec = pl.BlockSpec(g2_spec2.block_shape, lambda b, i, e: g2_spec2.index_map(b, i))
    return pl.pallas_call(
        functools.partial(_moe_kernel, final=final, precise=precise),
        grid=(bx, lx // tm, N_EXPERTS),
        in_specs=[
            tok3(d), tok3(LANES),
            pl.BlockSpec((None, d, D_EXPERT), lambda b, i, e: (layer, 0, e)),
            pl.BlockSpec((None, d, D_EXPERT), lambda b, i, e: (layer, 0, e)),
            pl.BlockSpec((None, D_EXPERT, d), lambda b, i, e: (layer, e, 0)),
            tok3(d),
            g2_spec,
            pl.BlockSpec((1, d), lambda b, i, e: (0, 0)),
        ],
        out_specs=tok3(d),
        out_shape=jax.ShapeDtypeStruct((bx, lx, d), F32),
        scratch_shapes=[pltpu.VMEM((tm, d), F32)],
        compiler_params=_cparams(3),
        name="hier_moe",
    )(h2, gates, wg, wu, wd, x1, mod.arr, g_final)


def _log_sigmoid(z):
    return -(jnp.maximum(-z, 0.0) + jnp.log1p(jnp.exp(-jnp.abs(z))))


def _kvq_kernel(x_ref, shk_ref, sck_ref, gkv_ref, wkv_ref, wf_ref, bf_ref, sh1_ref, sc1_ref, gmix_ref, wq_ref,
                k32_ref, v32_ref, k16_ref, v16_ref, lf_ref, q_ref, *, precise):
    x = x_ref[...]
    fd = FOX_HEADS * FOX_HD
    n = _rms_mod(x, gkv_ref[...], shk_ref[...], sck_ref[...])
    if not precise:
        n = n.astype(BF16)
    k = _mm(n, wkv_ref[:, 0:fd], precise)
    k32_ref[...] = k
    k16_ref[...] = k.astype(BF16)
    v = _mm(n, wkv_ref[:, fd:2 * fd], precise)
    v32_ref[...] = v
    v16_ref[...] = v.astype(BF16)
    z = _mm(n, wf_ref[...], precise) + bf_ref[...]
    lf_ref[...] = _log_sigmoid(z)[:, :FOX_HEADS]
    h = _rms_mod(x, gmix_ref[...], sh1_ref[...], sc1_ref[...])
    q_ref[...] = (_mm(h, wq_ref[...], precise) * (FOX_HD ** -0.5)).astype(q_ref.dtype)


def _kvq_call(x, kvmod, mod, g_kv, w_kv, w_f, b_f, g_mix, w_q, tm, precise):
    bx, lx, d = x.shape
    fd = FOX_HEADS * FOX_HD
    return pl.pallas_call(
        functools.partial(_kvq_kernel, precise=precise),
        grid=(bx, lx // tm),
        in_specs=[
            _tok_spec(tm, d),
            kvmod.spec(0, 0), kvmod.spec(0, 1),
            _const_spec((1, d)),
            _const_spec((d, 2 * fd)),
            _const_spec((d, LANES)),
            _const_spec((1, LANES)),
            mod.spec(1, 0), mod.spec(1, 1),
            _const_spec((1, d)),
            _const_spec((d, fd)),
        ],
        out_specs=[_tok_spec(tm, fd), _tok_spec(tm, fd), _tok_spec(tm, fd), _tok_spec(tm, fd),
                   _tok_spec(tm, FOX_HEADS), _tok_spec(tm, fd)],
        out_shape=[jax.ShapeDtypeStruct((bx, lx, fd), F32), jax.ShapeDtypeStruct((bx, lx, fd), F32),
                   jax.ShapeDtypeStruct((bx, lx, fd), BF16), jax.ShapeDtypeStruct((bx, lx, fd), BF16),
                   jax.ShapeDtypeStruct((bx, lx, FOX_HEADS), F32),
                   jax.ShapeDtypeStruct((bx, lx, fd), F32 if precise else BF16)],
        compiler_params=_cparams(2),
        name="kv_q_proj",
    )(x, kvmod.arr, kvmod.arr, g_kv, w_kv, w_f, b_f, mod.arr, mod.arr, g_mix, w_q)


def _cumsum_kernel(x_ref, o_ref):
    nh, l = x_ref.shape
    r = lax.broadcasted_iota(jnp.int32, (LANES, LANES), 0)
    c = lax.broadcasted_iota(jnp.int32, (LANES, LANES), 1)
    upper = (r <= c).astype(BF16)
    carry = jnp.zeros((nh, 1), F32)
    for blk in range(l // LANES):
        hi, mid, lo = _split3(x_ref[:, blk * LANES:(blk + 1) * LANES])
        cs = (_dot(hi, upper) + _dot(mid, upper)) + _dot(lo, upper) + carry
        o_ref[:, blk * LANES:(blk + 1) * LANES] = cs
        carry = cs[:, LANES - 1:LANES]


def _cumsum_call(x):
    b, nh, l = x.shape
    return pl.pallas_call(
        _cumsum_kernel,
        grid=(b,),
        in_specs=[pl.BlockSpec((None, nh, l), lambda i: (i, 0, 0))],
        out_specs=pl.BlockSpec((None, nh, l), lambda i: (i, 0, 0)),
        out_shape=jax.ShapeDtypeStruct((b, nh, l), F32),
        compiler_params=_cparams(1),
        name="logf_cumsum",
    )(x)


def _fox_kernel(q_ref, k_ref, v_ref, fq_ref, fk_ref, o_ref, acc_ref, *, t):
    qi = pl.program_id(2)
    first = lax.broadcasted_iota(jnp.int32, (1, LANES), 1) < FOX_HD
    col_a = lax.broadcasted_iota(jnp.int32, (1, 2 * t), 1) < t
    q2 = q_ref[...]
    zero = jnp.zeros_like(q2)
    qs = jnp.concatenate([jnp.where(first, q2, zero), jnp.where(first, zero, q2)], axis=0)
    fq = jnp.concatenate([fq_ref[0:1, :], fq_ref[1:2, :]], axis=1)
    key = lax.broadcasted_iota(jnp.int32, (t, 2 * t), 0)
    qry = lax.broadcasted_iota(jnp.int32, (t, 2 * t), 1)
    causal = key <= jnp.where(col_a, qry, qry - t)
    acc_ref[...] = jnp.zeros_like(acc_ref)

    def tile(j, m, l, diagonal):
        start = pl.multiple_of(j * t, t)
        kt = k_ref[pl.ds(start, t), :]
        vt = v_ref[pl.ds(start, t), :]
        fk = fk_ref[pl.ds(start, t), :]
        s = (_dot_nt(kt, qs) + fq) - jnp.where(col_a, fk[:, 0:1], fk[:, 1:2])
        if diagonal:
            s = jnp.where(causal, s, NEG_INF)
        m_new = jnp.maximum(m, jnp.max(s, axis=0, keepdims=True))
        p = jnp.exp(s - m_new)
        alpha = jnp.exp(m - m_new)
        acc_ref[...] = alpha * acc_ref[...] + _dot_tn(vt, p.astype(BF16))
        return m_new, alpha * l + jnp.sum(p, axis=0, keepdims=True)

    m0 = jnp.full((1, 2 * t), -jnp.inf, F32)
    l0 = jnp.zeros((1, 2 * t), F32)
    m, l = lax.fori_loop(0, qi, lambda j, c: tile(j, c[0], c[1], False), (m0, l0))
    m, l = tile(qi, m, l, True)
    o_t = acc_ref[...] / l
    row_a = lax.broadcasted_iota(jnp.int32, (LANES, 1), 0) < FOX_HD
    o_ref[...] = jnp.where(row_a, o_t[:, :t], o_t[:, t:]).T.astype(BF16)


def _fox_prompt_call(q, k16, v16, f_rows, f_cols, tq):
    b, l, fd = q.shape
    npair = fd // LANES
    return pl.pallas_call(
        functools.partial(_fox_kernel, t=tq),
        grid=(b, npair, l // tq),
        in_specs=[
            pl.BlockSpec((None, tq, LANES), lambda bb, hp, i: (bb, i, hp)),
            pl.BlockSpec((None, l, LANES), lambda bb, hp, i: (bb, 0, hp)),
            pl.BlockSpec((None, l, LANES), lambda bb, hp, i: (bb, 0, hp)),
            pl.BlockSpec((None, None, None, 2, tq), lambda bb, hp, i: (bb, hp, i, 0, 0)),
            pl.BlockSpec((None, None, l, 2), lambda bb, hp, i: (bb, hp, 0, 0)),
        ],
        out_specs=pl.BlockSpec((None, tq, LANES), lambda bb, hp, i: (bb, i, hp)),
        out_shape=jax.ShapeDtypeStruct((b, l, fd), BF16),
        scratch_shapes=[pltpu.VMEM((LANES, 2 * tq), F32)],
        compiler_params=_cparams(3),
        name="fox_attention_prompt",
    )(q, k16, v16, f_rows, f_cols)


def _attn_sample_kernel(pt_ref, q_ref, kn_ref, vn_ref, lfn_ref, *refs, pps, page):
    ck = refs[0:pps]
    cv = refs[pps:2 * pps]
    clf = refs[2 * pps:3 * pps]
    o_ref = refs[3 * pps]
    m_ref, l_ref, acc_ref, carry_ref = refs[3 * pps + 1:]
    step = pl.program_id(1)
    nh, hd = FOX_HEADS, FOX_HD
    fd = nh * hd
    diag = (lax.broadcasted_iota(jnp.int32, (nh, fd), 1) // hd) == lax.broadcasted_iota(jnp.int32, (nh, fd), 0)
    q_bd = jnp.where(diag, jnp.broadcast_to(q_ref[...], (nh, fd)), 0.0)

    @pl.when(step == 0)
    def _():
        m_ref[...] = jnp.sum(q_bd * kn_ref[...], axis=-1, keepdims=True)
        l_ref[...] = jnp.ones_like(l_ref)
        acc_ref[...] = jnp.broadcast_to(vn_ref[...], (nh, fd))
        carry_ref[...] = lfn_ref[...]

    tt = lax.broadcasted_iota(jnp.int32, (page, 2 * page), 0)
    cc = lax.broadcasted_iota(jnp.int32, (page, 2 * page), 1)
    later = ((tt > cc) | (cc >= page)).astype(BF16)

    def split2(x):
        hi = x.astype(BF16)
        return jnp.concatenate([hi, (x - hi.astype(F32)).astype(BF16)], axis=0)

    q2 = split2(q_bd)
    lf_parts = [part for i in range(pps) for part in _split3(clf[i][...])]
    lfx_all = _dot(jnp.concatenate(lf_parts, axis=0), later)
    carry = carry_ref[...]
    scores = []
    for i in range(pps):
        base = 3 * nh * i
        lfx = (lfx_all[base:base + nh] + lfx_all[base + nh:base + 2 * nh]) + lfx_all[base + 2 * nh:base + 3 * nh]
        bias = lfx[:, :page] + carry
        carry = carry + lfx[:, page:page + 1]
        s2 = _dot(q2, ck[i][...].reshape(fd, page).astype(BF16))
        scores.append((s2[:nh] + s2[nh:]) + bias)
    carry_ref[...] = carry
    s = jnp.concatenate(scores, axis=1)
    m_old = m_ref[...]
    m_new = jnp.maximum(m_old, jnp.max(s, axis=-1, keepdims=True))
    p = jnp.exp(s - m_new)
    alpha = jnp.exp(m_old - m_new)
    m_ref[...] = m_new
    l_ref[...] = alpha * l_ref[...] + jnp.sum(p, axis=-1, keepdims=True)
    p2 = split2(p)
    pv = _dot_nt(p2[:, 0:page], cv[0][...].reshape(fd, page).astype(BF16))
    for i in range(1, pps):
        pv = pv + _dot_nt(p2[:, i * page:(i + 1) * page], cv[i][...].reshape(fd, page).astype(BF16))
    acc_ref[...] = alpha * acc_ref[...] + (pv[:nh] + pv[nh:])

    @pl.when(step == pl.num_programs(1) - 1)
    def _():
        o = acc_ref[...] / l_ref[...]
        o_ref[...] = jnp.sum(jnp.where(diag, o, 0.0), axis=0, keepdims=True)


def _attn_sample_call(page_table, q, k_new, v_new, lf_new, cache_kt, cache_vt, cache_lft, pps):
    nb, n_pages = page_table.shape
    n_pool, nh, hd, page = cache_kt.shape
    fd = nh * hd
    steps = n_pages // pps

    def page_map(i, nd):
        return lambda b, s, pt: (pt[b, n_pages - 1 - (s * pps + i)],) + (0,) * nd

    row = lambda r, w: pl.BlockSpec((None, r, w), lambda b, s, pt: (b, 0, 0))
    in_specs = [row(1, fd), row(1, fd), row(1, fd), row(nh, 1)]
    in_specs += [pl.BlockSpec((None, nh, hd, page), page_map(i, 3)) for i in range(pps)]
    in_specs += [pl.BlockSpec((None, nh, hd, page), page_map(i, 3)) for i in range(pps)]
    in_specs += [pl.BlockSpec((None, nh, page), page_map(i, 2)) for i in range(pps)]
    grid_spec = pltpu.PrefetchScalarGridSpec(
        num_scalar_prefetch=1,
        grid=(nb, steps),
        in_specs=in_specs,
        out_specs=row(1, fd),
        scratch_shapes=[pltpu.VMEM((nh, 1), F32), pltpu.VMEM((nh, 1), F32), pltpu.VMEM((nh, fd), F32),
                        pltpu.VMEM((nh, 1), F32)],
    )
    return pl.pallas_call(
        functools.partial(_attn_sample_kernel, pps=pps, page=page),
        grid_spec=grid_spec,
        out_shape=jax.ShapeDtypeStruct((nb, 1, fd), F32),
        compiler_params=_cparams(2),
        name="fox_attention_sample",
    )(page_table, q, k_new, v_new, lf_new, *([cache_kt] * pps), *([cache_vt] * pps), *([cache_lft] * pps))


def _rope_tables(pos):
    half = RET_DK // 2
    inv = ROPE_BASE ** (-jnp.arange(half, dtype=F32) / half)
    ang = pos.astype(F32)[:, None] * inv[None, :]
    return jnp.cos(ang), jnp.sin(ang)


def _pick_tile(n, pref):
    t = min(n, pref)
    while n % t:
        t //= 2
    return t


def kernel(x_prompt, x_sample, c_prompt, c_sample, state_ret, cache_k, cache_v, cache_logf, page_table, w_mod, b_mod, g_mix, g_ffn, w_ret_in, w_ret_out, g_kv, w_kvmod, b_kvmod, w_kv, w_f, b_f, w_fq, w_fo, w_rg, w_re, w_e_gate, w_e_up, w_e_down, g_final):
    b, l, d = x_prompt.shape
    nb = x_sample.shape[0]
    n_pool, page = cache_k.shape[0], cache_k.shape[1]
    n_pages = page_table.shape[1]
    past_len = n_pages * page
    fd = FOX_HEADS * FOX_HD

    c_all = jnp.concatenate([c_sample, c_prompt], axis=0)
    mod_all = _mod_call(c_all, w_mod, b_mod, 1536)
    kvmod_all = _mod_call(c_all, w_kvmod[None], b_kvmod[None], 1024)

    w_f_pad = jnp.pad(w_f, ((0, 0), (0, LANES - FOX_HEADS)))
    weights32 = dict(w_in=w_ret_in, w_out=w_ret_out[0], w_kv=w_kv, w_fq=w_fq[0], w_fo=w_fo[0], w_f=w_f_pad,
                     wg=w_e_gate, wu=w_e_up, wd=w_e_down)
    weights16 = {name: w.astype(BF16) for name, w in weights32.items()}
    b_f_row = jnp.pad(b_f, (0, LANES - FOX_HEADS)).reshape(1, LANES)
    w_router = jnp.concatenate([jnp.pad(w_rg, ((0, 0), (0, 0), (0, LANES - N_GROUPS))),
                                jnp.pad(w_re, ((0, 0), (0, 0), (0, LANES - N_EXPERTS)))], axis=-1)
    g_mix2 = g_mix.reshape(-1, 1, d)
    g_ffn2 = g_ffn.reshape(-1, 1, d)
    g_kv2 = g_kv.reshape(1, d)
    g_fin2 = g_final.reshape(1, d)

    def trunk(x, mod, kvmod, pos, tm, tm_in, tm_moe, mixer0, mixer1, precise):
        w = weights32 if precise else weights16
        cos, sin = _rope_tables(pos)
        proj = _retin_call(x, mod, g_mix2[0], w["w_in"], cos, sin, tm_in, precise)
        o, s_new = mixer0(proj)
        x1, h2, gates = _mixout_call(o, w["w_out"], x, mod, 0, g_ffn2[0], w_router[0], tm, precise)
        x2 = _moe_call(h2, gates, w["wg"], w["wu"], w["wd"], 0, x1, mod, g_fin2, tm_moe, False, precise)
        k32, v32, k16, v16, lf, qf = _kvq_call(x2, kvmod, mod, g_kv2, w["w_kv"], w["w_f"], b_f_row, g_mix2[1],
                                               w["w_fq"], tm, precise)
        o1 = mixer1(qf, k32, v32, k16, v16, lf)
        x3, h4, gates1 = _mixout_call(o1, w["w_fo"], x2, mod, 1, g_ffn2[1], w_router[1], tm, precise)
        y = _moe_call(h4, gates1, w["wg"], w["wu"], w["wd"], 1, x3, mod, g_fin2, tm_moe, True, precise)
        return y, s_new, k32, v32, lf

    tm_p = _pick_tile(l, 512)
    chunk = _pick_tile(l, 256)
    tq = _pick_tile(l, 512)

    def ret_prompt(proj):
        return _ret_prompt_call(proj, chunk)

    def fox_prompt(qf, k32, v32, k16, v16, lf):
        f_t = _cumsum_call(jnp.swapaxes(lf, 1, 2))
        f_pairs = f_t.reshape(b, FOX_HEADS // 2, 2, l)
        f_cols = jnp.swapaxes(f_pairs, 2, 3)
        f_rows = jnp.swapaxes(f_pairs.reshape(b, FOX_HEADS // 2, 2, l // tq, tq), 2, 3)
        return _fox_prompt_call(qf, k16, v16, f_rows, f_cols, tq)

    mod_p = _Mod(mod_all, False, nb, tm_p)
    kvmod_p = _Mod(kvmod_all, False, nb, tm_p)
    y_p, s_p, k_p, v_p, lf_p = trunk(x_prompt, mod_p, kvmod_p, jnp.arange(l), tm_p, _pick_tile(l, 1024),
                                     _pick_tile(l, 1024), ret_prompt, fox_prompt, False)

    def ret_sample(proj):
        o, s_new = _ret_sample_call(proj[0], state_ret[0])
        return o.reshape(1, nb, -1), s_new

    def fox_sample(qf, k32, v32, k16, v16, lf):
        rows = lambda a: a.reshape(nb, 1, fd)
        o = _attn_sample_call(page_table, rows(qf), rows(k32), rows(v32), lf.reshape(nb, FOX_HEADS, 1),
                              jnp.transpose(cache_k, (0, 2, 3, 1)), jnp.transpose(cache_v, (0, 2, 3, 1)),
                              jnp.swapaxes(cache_logf, 1, 2), _pick_tile(n_pages, 8))
        return o.reshape(1, nb, fd)

    mod_s = _Mod(mod_all, True, 0, nb)
    kvmod_s = _Mod(kvmod_all, True, 0, nb)
    y_s, s_s, k_s, v_s, lf_s = trunk(x_sample.reshape(1, nb, d), mod_s, kvmod_s,
                                     jnp.full((nb,), past_len, jnp.int32), nb, nb, nb, ret_sample, fox_sample, True)

    return (y_p, y_s.reshape(nb, 1, d), s_p[None], s_s[None],
            k_p.reshape(b, l, FOX_HEADS, FOX_HD), v_p.reshape(b, l, FOX_HEADS, FOX_HD), lf_p,
            k_s.reshape(nb, 1, FOX_HEADS, FOX_HD), v_s.reshape(nb, 1, FOX_HEADS, FOX_HD),
            lf_s.reshape(nb, 1, FOX_HEADS))
```

```python
import functools
import math

import jax
import jax.numpy as jnp
from jax import lax
from jax.experimental import pallas as pl
from jax.experimental.pallas import tpu as pltpu

D_MODEL = 1024
RET_HEADS = 4
RET_DK = 256
RET_DV = 512
ROPE_BASE = 10000.0
FOX_HEADS = 16
FOX_HD = 64
N_GROUPS = 4
EXPERTS_PER_GROUP = 8
N_EXPERTS = 32
D_EXPERT = 256
EPS = 1e-6
NEG_INF = -1e30

LANES = 128
VMEM_LIMIT = 56 * 1024 * 1024
GROUP_ID_LANE = 64
MOE_CHUNK = 128
MOE_WINDOW = 3
F32 = jnp.float32
BF16 = jnp.bfloat16


def _cparams(n_axes):
    return pltpu.CompilerParams(dimension_semantics=("arbitrary",) * n_axes,
                                vmem_limit_bytes=VMEM_LIMIT)


def _silu(x):
    return x * jax.nn.sigmoid(x)


def _dot(a, b):
    return jnp.dot(a, b, preferred_element_type=F32)


def _dot_nt(a, b):
    return lax.dot_general(a, b, (((1,), (1,)), ((), ())), preferred_element_type=F32)


def _dot_tn(a, b):
    return lax.dot_general(a, b, (((0,), (0,)), ((), ())), preferred_element_type=F32)


def _mm(a, w, precise):
    if precise:
        return jnp.dot(a, w, preferred_element_type=F32, precision=lax.Precision.HIGHEST)
    return _dot(a.astype(BF16), w)


def _split3(x):
    hi = x.astype(BF16)
    r1 = x - hi.astype(F32)
    mid = r1.astype(BF16)
    lo = (r1 - mid.astype(F32)).astype(BF16)
    return hi, mid, lo


def _rms_mod(x, g, shift, scale):
    y = x * lax.rsqrt(jnp.mean(x * x, axis=-1, keepdims=True) + EPS)
    return (y * g) * (1.0 + scale) + shift


def _mod_kernel(c_ref, w_ref, b_ref, o_ref):
    o_ref[...] = _mm(_silu(c_ref[...]), w_ref[...], True) + b_ref[...]


def _mod_call(c, w, b, tn):
    ns, k, n = w.shape
    m = c.shape[0]
    return pl.pallas_call(
        _mod_kernel,
        grid=(ns, n // tn),
        in_specs=[
            pl.BlockSpec((m, k), lambda s, j: (0, 0)),
            pl.BlockSpec((None, k, tn), lambda s, j: (s, 0, j)),
            pl.BlockSpec((None, 1, tn), lambda s, j: (s, 0, j)),
        ],
        out_specs=pl.BlockSpec((None, m, tn), lambda s, j: (s, 0, j)),
        out_shape=jax.ShapeDtypeStruct((ns, m, n), F32),
        compiler_params=_cparams(2),
        name="adaln_mod",
    )(c, w, b.reshape(ns, 1, n))


class _Mod:
    def __init__(self, arr, per_token, row0, tm):
        self.per_token = per_token
        self.row0 = row0
        self.tm = tm
        self.arr = arr if per_token else arr.reshape(arr.shape[0], arr.shape[1], 1, arr.shape[2])

    def spec(self, stack, col):
        if self.per_token:
            assert self.row0 % self.tm == 0
            r0 = self.row0 // self.tm
            return pl.BlockSpec((None, self.tm, D_MODEL), lambda b, i: (stack, r0 + i, col))
        row0 = self.row0
        return pl.BlockSpec((None, None, 1, D_MODEL), lambda b, i: (stack, row0 + b, 0, col))


def _tok_spec(tm, width):
    return pl.BlockSpec((None, tm, width), lambda b, i: (b, i, 0))


def _const_spec(shape):
    nd = len(shape)
    return pl.BlockSpec(shape, lambda b, i: (0,) * nd)


def _retin_kernel(x_ref, sh_ref, sc_ref, g_ref, w_ref, cos_ref, sin_ref, o_ref, h_ref, *, precise):
    j = pl.program_id(2)

    @pl.when(j == 0)
    def _():
        h_ref[...] = _rms_mod(x_ref[...], g_ref[...], sh_ref[...], sc_ref[...]).astype(h_ref.dtype)

    p = _mm(h_ref[...], w_ref[...], precise)

    @pl.when(j < 2)
    def _():
        scale = jnp.where(j == 0, 1.0, RET_DK ** -0.5)
        half = RET_DK // 2
        cos = cos_ref[...]
        sin = sin_ref[...]
        for hh in range(RET_HEADS):
            a = hh * RET_DK
            x1 = p[:, a:a + half]
            x2 = p[:, a + half:a + RET_DK]
            o_ref[:, a:a + half] = ((x1 * cos - x2 * sin) * scale).astype(o_ref.dtype)
            o_ref[:, a + half:a + RET_DK] = ((x1 * sin + x2 * cos) * scale).astype(o_ref.dtype)

    @pl.when(j >= 2)
    def _():
        o_ref[...] = p.astype(o_ref.dtype)


def _retin_call(x, mod, g_mix, w_in, cos, sin, tm, precise):
    bx, lx, d = x.shape
    tn = RET_HEADS * RET_DK
    n = w_in.shape[-1]
    dt = F32 if precise else BF16
    spec3 = lambda sp: pl.BlockSpec(sp.block_shape, lambda b, i, j: sp.index_map(b, i))
    return pl.pallas_call(
        functools.partial(_retin_kernel, precise=precise),
        grid=(bx, lx // tm, n // tn),
        in_specs=[
            spec3(_tok_spec(tm, d)),
            spec3(mod.spec(0, 0)), spec3(mod.spec(0, 1)),
            spec3(_const_spec((1, d))),
            pl.BlockSpec((None, d, tn), lambda b, i, j: (0, 0, j)),
            pl.BlockSpec((tm, RET_DK // 2), lambda b, i, j: (i, 0)),
            pl.BlockSpec((tm, RET_DK // 2), lambda b, i, j: (i, 0)),
        ],
        out_specs=pl.BlockSpec((None, tm, tn), lambda b, i, j: (b, i, j)),
        out_shape=jax.ShapeDtypeStruct((bx, lx, n), dt),
        scratch_shapes=[pltpu.VMEM((tm, d), dt)],
        compiler_params=_cparams(3),
        name="ret_in_proj",
    )(x, mod.arr, mod.arr, g_mix, w_in, cos, sin)


def _log_gamma(h):
    return math.log(1.0 - 2.0 ** (-5.0 - h))


def _ret_prompt_kernel(q_ref, k_ref, v_ref, g_ref, o_ref, s_ref, *, chunk):
    @pl.when(pl.program_id(1) == 0)
    def _():
        s_ref[...] = jnp.zeros_like(s_ref)

    ti = lax.broadcasted_iota(jnp.int32, (chunk, chunk), 0)
    tj = lax.broadcasted_iota(jnp.int32, (chunk, chunk), 1)
    rel = (ti - tj).astype(F32)
    t = lax.broadcasted_iota(jnp.int32, (chunk, 1), 0).astype(F32)
    for h in range(RET_HEADS):
        lg = _log_gamma(h)
        decay = jnp.where(rel >= 0, jnp.exp(lg * jnp.maximum(rel, 0.0)), 0.0)
        qh = q_ref[:, h * RET_DK:(h + 1) * RET_DK]
        kh = k_ref[:, h * RET_DK:(h + 1) * RET_DK]
        vh = v_ref[:, h * RET_DV:(h + 1) * RET_DV]
        a = _dot_nt(qh, kh) * decay
        inner = _dot(a.astype(BF16), vh)
        s_old = s_ref[h]
        cross = _dot(qh, s_old.astype(BF16)) * jnp.exp((t + 1.0) * lg)
        o = inner + cross
        kd = (kh.astype(F32) * jnp.exp((chunk - 1.0 - t) * lg)).astype(BF16)
        s_ref[h] = math.exp(chunk * lg) * s_old + _dot_tn(kd, vh)
        o = o * lax.rsqrt(jnp.mean(o * o, axis=-1, keepdims=True) + EPS)
        gh = g_ref[:, h * RET_DV:(h + 1) * RET_DV].astype(F32)
        o_ref[:, h * RET_DV:(h + 1) * RET_DV] = (o * _silu(gh)).astype(BF16)


def _ret_prompt_call(proj, chunk):
    b, l, _ = proj.shape
    qd = RET_HEADS * RET_DK
    vd = RET_HEADS * RET_DV
    col = lambda w, j: pl.BlockSpec((None, chunk, w), lambda bb, c: (bb, c, j))
    return pl.pallas_call(
        functools.partial(_ret_prompt_kernel, chunk=chunk),
        grid=(b, l // chunk),
        in_specs=[col(qd, 0), col(qd, 1), col(vd, 1), col(vd, 2)],
        out_specs=[_tok_spec(chunk, vd),
                   pl.BlockSpec((None, RET_HEADS, RET_DK, RET_DV), lambda bb, c: (bb, 0, 0, 0))],
        out_shape=[jax.ShapeDtypeStruct((b, l, vd), BF16),
                   jax.ShapeDtypeStruct((b, RET_HEADS, RET_DK, RET_DV), F32)],
        compiler_params=_cparams(2),
        name="retention_prompt",
    )(proj, proj, proj, proj)


def _ret_sample_kernel(q_ref, k_ref, v_ref, g_ref, s_ref, o_ref, sn_ref):
    b = pl.program_id(0)
    h = pl.program_id(1)
    nb = q_ref.shape[0]
    gamma = 1.0 - 1.0 / (jnp.zeros((1, 1), F32) + (jnp.int32(32) << h).astype(F32))
    rowsel = lax.broadcasted_iota(jnp.int32, (nb, 1), 0) == b
    q = jnp.where(rowsel, q_ref[...], 0.0)
    k = jnp.where(rowsel, k_ref[...], 0.0)
    v = v_ref[...]
    vb = jnp.sum(jnp.where(rowsel, v, 0.0), axis=0, keepdims=True)
    gb = jnp.sum(jnp.where(rowsel, g_ref[...], 0.0), axis=0, keepdims=True)
    s_old = s_ref[...]
    kv = lax.dot_general(k, v, (((0,), (0,)), ((), ())), preferred_element_type=F32,
                         precision=lax.Precision.HIGHEST)
    sn_ref[...] = gamma * s_old + kv
    qk = jnp.sum(jnp.sum(q * k, axis=1, keepdims=True), axis=0, keepdims=True)
    cross = jnp.sum(_mm(q, s_old, True), axis=0, keepdims=True) * gamma
    o = qk * vb + cross
    o = o * lax.rsqrt(jnp.mean(o * o, axis=-1, keepdims=True) + EPS)
    o_ref[...] = o * _silu(gb)


def _ret_sample_call(proj, s0):
    nb = proj.shape[0]
    k0 = RET_HEADS
    v0 = 2 * RET_HEADS * RET_DK // RET_DV
    g0 = v0 + RET_HEADS
    return pl.pallas_call(
        _ret_sample_kernel,
        grid=(nb, RET_HEADS),
        in_specs=[
            pl.BlockSpec((nb, RET_DK), lambda b, h: (0, h)),
            pl.BlockSpec((nb, RET_DK), lambda b, h: (0, k0 + h)),
            pl.BlockSpec((nb, RET_DV), lambda b, h: (0, v0 + h)),
            pl.BlockSpec((nb, RET_DV), lambda b, h: (0, g0 + h)),
            pl.BlockSpec((None, None, RET_DK, RET_DV), lambda b, h: (b, h, 0, 0)),
        ],
        out_specs=[pl.BlockSpec((None, 1, RET_DV), lambda b, h: (b, 0, h)),
                   pl.BlockSpec((None, None, RET_DK, RET_DV), lambda b, h: (b, h, 0, 0))],
        out_shape=[jax.ShapeDtypeStruct((nb, 1, RET_HEADS * RET_DV), F32),
                   jax.ShapeDtypeStruct((nb, RET_HEADS, RET_DK, RET_DV), F32)],
        compiler_params=_cparams(2),
        name="retention_sample",
    )(proj, proj, proj, proj, s0)


def _route(logits):
    tm = logits.shape[0]
    lane = lax.broadcasted_iota(jnp.int32, (tm, LANES), 1).astype(F32)
    ninf = -jnp.inf
    lg = jnp.where(lane < N_GROUPS, logits[:, :LANES], ninf)
    mx = jnp.max(lg, axis=-1, keepdims=True)
    p_group = 1.0 / jnp.sum(jnp.exp(lg - mx), axis=-1, keepdims=True)
    gi = jnp.min(jnp.where(lg == mx, lane, float(LANES)), axis=-1, keepdims=True)
    lo = gi * EXPERTS_PER_GROUP
    les = jnp.where((lane >= lo) & (lane < lo + EXPERTS_PER_GROUP), logits[:, LANES:], ninf)
    v1 = jnp.max(les, axis=-1, keepdims=True)
    i1 = jnp.min(jnp.where(les == v1, lane, float(LANES)), axis=-1, keepdims=True)
    les2 = jnp.where(lane == i1, ninf, les)
    v2 = jnp.max(les2, axis=-1, keepdims=True)
    i2 = jnp.min(jnp.where(les2 == v2, lane, float(LANES)), axis=-1, keepdims=True)
    e2 = jnp.exp(v2 - v1)
    w1 = 1.0 / (1.0 + e2)
    w2 = e2 / (1.0 + e2)
    gates = jnp.where(lane == i1, w1 * p_group, 0.0) + jnp.where(lane == i2, w2 * p_group, 0.0)
    return gates + jnp.where(lane == GROUP_ID_LANE, gi, 0.0)


def _mixout_kernel(o_ref, w_ref, x_ref, g1_ref, sh_ref, sc_ref, gf_ref, wr_ref, x1_ref, h2_ref, gates_ref, *,
                   precise):
    y = _mm(o_ref[...], w_ref[...], precise)
    x1 = x_ref[...] + g1_ref[...] * y
    x1_ref[...] = x1
    h2 = _rms_mod(x1, gf_ref[...], sh_ref[...], sc_ref[...])
    h2_ref[...] = h2.astype(h2_ref.dtype)
    gates_ref[...] = _route(_mm(h2, wr_ref[...], True))


def _mixout_call(o, w_out, x, mod, layer, g_ffn, w_router, tm, precise):
    bx, lx, d = x.shape
    kd = o.shape[-1]
    return pl.pallas_call(
        functools.partial(_mixout_kernel, precise=precise),
        grid=(bx, lx // tm),
        in_specs=[
            _tok_spec(tm, kd),
            _const_spec((kd, d)),
            _tok_spec(tm, d),
            mod.spec(layer, 2), mod.spec(layer, 3), mod.spec(layer, 4),
            _const_spec((1, d)),
            _const_spec((d, 2 * LANES)),
        ],
        out_specs=[_tok_spec(tm, d), _tok_spec(tm, d), _tok_spec(tm, LANES)],
        out_shape=[jax.ShapeDtypeStruct((bx, lx, d), F32),
                   jax.ShapeDtypeStruct((bx, lx, d), F32 if precise else BF16),
                   jax.ShapeDtypeStruct((bx, lx, LANES), F32)],
        compiler_params=_cparams(2),
        name="mix_out_router",
    )(o, w_out, x, mod.arr, mod.arr, mod.arr, g_ffn, w_router)


def _moe_kernel(h_ref, gates_ref, wg_ref, wu_ref, wd_ref, x1_ref, g2_ref, gfin_ref, out_ref, acc_ref, *, final,
                precise):
    e = pl.program_id(2)

    @pl.when(e == 0)
    def _():
        acc_ref[...] = jnp.zeros_like(acc_ref)

    h = h_ref[...]
    a = _mm(h, wg_ref[...], precise)
    u = _mm(h, wu_ref[...], precise)
    lane = lax.broadcasted_iota(jnp.int32, gates_ref.shape, 1)
    ge = jnp.sum(jnp.where(lane == e, gates_ref[...], 0.0), axis=-1, keepdims=True)
    act = (_silu(a) * u) * ge
    acc_ref[...] += _mm(act, wd_ref[...], precise)

    @pl.when(e == pl.num_programs(2) - 1)
    def _():
        x2 = x1_ref[...] + g2_ref[...] * acc_ref[...]
        if final:
            x2 = (x2 * lax.rsqrt(jnp.mean(x2 * x2, axis=-1, keepdims=True) + EPS)) * gfin_ref[...]
        out_ref[...] = x2


def _moe_call(h2, gates, wg, wu, wd, layer, x1, mod, g_final, tm, final, precise):
    bx, lx, d = x1.shape
    tok3 = lambda w: pl.BlockSpec((None, tm, w), lambda b, i, e: (b, i, 0))
    g2_spec2 = mod.spec(layer, 5)
    g2_spec = pl.BlockSpec(g2_spec2.block_shape, lambda b, i, e: g2_spec2.index_map(b, i))
    return pl.pallas_call(
        functools.partial(_moe_kernel, final=final, precise=precise),
        grid=(bx, lx // tm, N_EXPERTS),
        in_specs=[
            tok3(d), tok3(LANES),
            pl.BlockSpec((None, d, D_EXPERT), lambda b, i, e: (layer, 0, e)),
            pl.BlockSpec((None, d, D_EXPERT), lambda b, i, e: (layer, 0, e)),
            pl.BlockSpec((None, D_EXPERT, d), lambda b, i, e: (layer, e, 0)),
            tok3(d),
            g2_spec,
            pl.BlockSpec((1, d), lambda b, i, e: (0, 0)),
        ],
        out_specs=tok3(d),
        out_shape=jax.ShapeDtypeStruct((bx, lx, d), F32),
        scratch_shapes=[pltpu.VMEM((tm, d), F32)],
        compiler_params=_cparams(3),
        name="hier_moe",
    )(h2, gates, wg, wu, wd, x1, mod.arr, g_final)


def _moe_plan_kernel(gates_ref, pos_ref, gs_ref, flags_ref, *, chunk):
    g = gates_ref[...]
    tm = g.shape[0]
    lane = lax.broadcasted_iota(jnp.int32, (tm, LANES), 1)
    gid = jnp.sum(jnp.where(lane == GROUP_ID_LANE, g, 0.0), axis=-1, keepdims=True)
    onehot = jnp.where((lane.astype(F32) == gid) & (lane < N_GROUPS), 1.0, 0.0)
    r = lax.broadcasted_iota(jnp.int32, (tm, tm), 0)
    c = lax.broadcasted_iota(jnp.int32, (tm, tm), 1)
    before = _dot((c < r).astype(BF16), onehot.astype(BF16))
    rank = jnp.sum(onehot * before, axis=-1, keepdims=True)
    count = jnp.broadcast_to(jnp.sum(onehot, axis=0, keepdims=True), (8, LANES))
    lower = (lax.broadcasted_iota(jnp.int32, (LANES, LANES), 0)
             < lax.broadcasted_iota(jnp.int32, (LANES, LANES), 1)).astype(BF16)
    hi, mid, lo = _split3(count)
    start = (_dot(hi, lower) + _dot(mid, lower)) + _dot(lo, lower)
    pos = jnp.sum(onehot * start[0:1], axis=-1, keepdims=True) + rank
    pos_ref[...] = jnp.broadcast_to(pos, (tm, LANES))
    place = (pos == c.astype(F32)).astype(BF16)
    hi, mid, lo = _split3(jnp.where(lane < N_EXPERTS, g, 0.0))
    gs = (_dot_tn(place, hi) + _dot_tn(place, mid)) + _dot_tn(place, lo)
    gs_ref[...] = gs
    flags_ref[...] = jnp.zeros_like(flags_ref)
    for k in range(tm // chunk):
        used = jnp.max(jnp.where(gs[k * chunk:(k + 1) * chunk] != 0.0, 1.0, 0.0), axis=0, keepdims=True)
        flags_ref[k:k + 1, :] = used.astype(jnp.int32)


def _moe_plan_call(gates, tm, chunk):
    bx, lx, _ = gates.shape
    n_chunks = tm // chunk
    rows = -(-n_chunks // 8) * 8
    tok = lambda: pl.BlockSpec((None, tm, LANES), lambda b, i: (b, i, 0))
    return pl.pallas_call(
        functools.partial(_moe_plan_kernel, chunk=chunk),
        grid=(bx, lx // tm),
        in_specs=[tok()],
        out_specs=[tok(), tok(), pl.BlockSpec((None, None, rows, LANES), lambda b, i: (b, i, 0, 0))],
        out_shape=[jax.ShapeDtypeStruct((bx, lx, LANES), F32), jax.ShapeDtypeStruct((bx, lx, LANES), F32),
                   jax.ShapeDtypeStruct((bx, lx // tm, rows, LANES), jnp.int32)],
        compiler_params=_cparams(2),
        name="moe_plan",
    )(gates)


def _moe_sorted_kernel(first_ref, nwin_ref, h_ref, pos_ref, gs_ref, wg_ref, wu_ref, wd_ref, x1_ref, g2_ref, gfin_ref,
                       out_ref, place_ref, xs_ref, gsp_ref, acc_ref, *, final, chunk, window):
    b = pl.program_id(0)
    i = pl.program_id(1)
    e = pl.program_id(2)
    tm = h_ref.shape[0]
    tile = b * pl.num_programs(1) + i

    @pl.when(e == 0)
    def _():
        dest = lax.broadcasted_iota(jnp.int32, (tm, tm), 1).astype(F32)
        place = (pos_ref[:, 0:1] == dest).astype(BF16)
        place_ref[...] = place
        xs_ref[...] = jnp.zeros_like(xs_ref)
        xs_ref[0:tm, :] = _dot_tn(place, h_ref[...]).astype(BF16)
        gsp_ref[...] = jnp.zeros_like(gsp_ref)
        gsp_ref[0:tm, :] = gs_ref[...]
        acc_ref[...] = jnp.zeros_like(acc_ref)

    wrows = window * chunk
    lane = lax.broadcasted_iota(jnp.int32, (wrows, LANES), 1)

    def one_window(w, carry):
        rows = pl.ds(pl.multiple_of((first_ref[tile, e] + w * window) * chunk, chunk), wrows)
        x = xs_ref[rows, :]
        ge = jnp.sum(jnp.where(lane == e, gsp_ref[rows, :], 0.0), axis=-1, keepdims=True)
        act = (_silu(_dot(x, wg_ref[...])) * _dot(x, wu_ref[...])) * ge
        acc_ref[rows, :] += _dot(act.astype(BF16), wd_ref[...])
        return carry

    lax.fori_loop(0, nwin_ref[tile, e], one_window, 0)

    @pl.when(e == pl.num_programs(2) - 1)
    def _():
        acc = acc_ref[0:tm, :]
        hi = acc.astype(BF16)
        lo = (acc - hi.astype(F32)).astype(BF16)
        place = place_ref[...]
        x2 = x1_ref[...] + g2_ref[...] * (_dot(place, hi) + _dot(place, lo))
        if final:
            x2 = (x2 * lax.rsqrt(jnp.mean(x2 * x2, axis=-1, keepdims=True) + EPS)) * gfin_ref[...]
        out_ref[...] = x2


def _moe_sorted_call(h2, gates, wg, wu, wd, layer, x1, mod, g_final, tm, final):
    bx, lx, d = x1.shape
    chunk = min(tm, MOE_CHUNK)
    n_chunks = tm // chunk
    window = min(n_chunks, MOE_WINDOW)
    pos, gs, flags = _moe_plan_call(gates, tm, chunk)
    used = flags[:, :, :n_chunks, :N_EXPERTS].reshape(-1, n_chunks, N_EXPERTS) != 0
    idx = jnp.arange(n_chunks, dtype=jnp.int32)[None, :, None]
    first = jnp.min(jnp.where(used, idx, n_chunks), axis=1)
    last = jnp.max(jnp.where(used, idx, -1), axis=1)
    nwin = jnp.where(last >= 0, (last - first) // window + 1, 0).astype(jnp.int32)
    first = jnp.where(last >= 0, first, 0).astype(jnp.int32)
    pad_rows = (window - 1) * chunk
    tok3 = lambda w: pl.BlockSpec((None, tm, w), lambda b, i, e, f, n: (b, i, 0))
    g2_spec2 = mod.spec(layer, 5)
    g2_spec = pl.BlockSpec(g2_spec2.block_shape, lambda b, i, e, f, n: g2_spec2.index_map(b, i))
    grid_spec = pltpu.PrefetchScalarGridSpec(
        num_scalar_prefetch=2,
        grid=(bx, lx // tm, N_EXPERTS),
        in_specs=[
            tok3(d), tok3(LANES), tok3(LANES),
            pl.BlockSpec((None, d, D_EXPERT), lambda b, i, e, f, n: (layer, 0, e)),
            pl.BlockSpec((None, d, D_EXPERT), lambda b, i, e, f, n: (layer, 0, e)),
            pl.BlockSpec((None, D_EXPERT, d), lambda b, i, e, f, n: (layer, e, 0)),
            tok3(d),
            g2_spec,
            pl.BlockSpec((1, d), lambda b, i, e, f, n: (0, 0)),
        ],
        out_specs=tok3(d),
        scratch_shapes=[pltpu.VMEM((tm, tm), BF16), pltpu.VMEM((tm + pad_rows, d), BF16),
                        pltpu.VMEM((tm + pad_rows, LANES), F32), pltpu.VMEM((tm + pad_rows, d), F32)],
    )
    return pl.pallas_call(
        functools.partial(_moe_sorted_kernel, final=final, chunk=chunk, window=window),
        grid_spec=grid_spec,
        out_shape=jax.ShapeDtypeStruct((bx, lx, d), F32),
        compiler_params=_cparams(3),
        name="hier_moe_sorted",
    )(first, nwin, h2, pos, gs, wg, wu, wd, x1, mod.arr, g_final)


def _log_sigmoid(z):
    return -(jnp.maximum(-z, 0.0) + jnp.log1p(jnp.exp(-jnp.abs(z))))


def _kvq_kernel(x_ref, shk_ref, sck_ref, gkv_ref, wkv_ref, wf_ref, bf_ref, sh1_ref, sc1_ref, gmix_ref, wq_ref,
                k32_ref, v32_ref, k16_ref, v16_ref, lf_ref, q_ref, *, precise):
    x = x_ref[...]
    fd = FOX_HEADS * FOX_HD
    n = _rms_mod(x, gkv_ref[...], shk_ref[...], sck_ref[...])
    if not precise:
        n = n.astype(BF16)
    k = _mm(n, wkv_ref[:, 0:fd], precise)
    k32_ref[...] = k
    k16_ref[...] = k.astype(BF16)
    v = _mm(n, wkv_ref[:, fd:2 * fd], precise)
    v32_ref[...] = v
    v16_ref[...] = v.astype(BF16)
    z = _mm(n, wf_ref[...], precise) + bf_ref[...]
    lf_ref[...] = _log_sigmoid(z)[:, :FOX_HEADS]
    h = _rms_mod(x, gmix_ref[...], sh1_ref[...], sc1_ref[...])
    q_ref[...] = (_mm(h, wq_ref[...], precise) * (FOX_HD ** -0.5)).astype(q_ref.dtype)


def _kvq_call(x, kvmod, mod, g_kv, w_kv, w_f, b_f, g_mix, w_q, tm, precise):
    bx, lx, d = x.shape
    fd = FOX_HEADS * FOX_HD
    return pl.pallas_call(
        functools.partial(_kvq_kernel, precise=precise),
        grid=(bx, lx // tm),
        in_specs=[
            _tok_spec(tm, d),
            kvmod.spec(0, 0), kvmod.spec(0, 1),
            _const_spec((1, d)),
            _const_spec((d, 2 * fd)),
            _const_spec((d, LANES)),
            _const_spec((1, LANES)),
            mod.spec(1, 0), mod.spec(1, 1),
            _const_spec((1, d)),
            _const_spec((d, fd)),
        ],
        out_specs=[_tok_spec(tm, fd), _tok_spec(tm, fd), _tok_spec(tm, fd), _tok_spec(tm, fd),
                   _tok_spec(tm, FOX_HEADS), _tok_spec(tm, fd)],
        out_shape=[jax.ShapeDtypeStruct((bx, lx, fd), F32), jax.ShapeDtypeStruct((bx, lx, fd), F32),
                   jax.ShapeDtypeStruct((bx, lx, fd), BF16), jax.ShapeDtypeStruct((bx, lx, fd), BF16),
                   jax.ShapeDtypeStruct((bx, lx, FOX_HEADS), F32),
                   jax.ShapeDtypeStruct((bx, lx, fd), F32 if precise else BF16)],
        compiler_params=_cparams(2),
        name="kv_q_proj",
    )(x, kvmod.arr, kvmod.arr, g_kv, w_kv, w_f, b_f, mod.arr, mod.arr, g_mix, w_q)


def _cumsum_kernel(x_ref, o_ref):
    nh, l = x_ref.shape
    r = lax.broadcasted_iota(jnp.int32, (LANES, LANES), 0)
    c = lax.broadcasted_iota(jnp.int32, (LANES, LANES), 1)
    upper = (r <= c).astype(BF16)
    carry = jnp.zeros((nh, 1), F32)
    for blk in range(l // LANES):
        hi, mid, lo = _split3(x_ref[:, blk * LANES:(blk + 1) * LANES])
        cs = (_dot(hi, upper) + _dot(mid, upper)) + _dot(lo, upper) + carry
        o_ref[:, blk * LANES:(blk + 1) * LANES] = cs
        carry = cs[:, LANES - 1:LANES]


def _cumsum_call(x):
    b, nh, l = x.shape
    return pl.pallas_call(
        _cumsum_kernel,
        grid=(b,),
        in_specs=[pl.BlockSpec((None, nh, l), lambda i: (i, 0, 0))],
        out_specs=pl.BlockSpec((None, nh, l), lambda i: (i, 0, 0)),
        out_shape=jax.ShapeDtypeStruct((b, nh, l), F32),
        compiler_params=_cparams(1),
        name="logf_cumsum",
    )(x)


def _fox_kernel(q_ref, k_ref, v_ref, fq_ref, fk_ref, o_ref, acc_ref, *, t):
    qi = pl.program_id(2)
    first = lax.broadcasted_iota(jnp.int32, (1, LANES), 1) < FOX_HD
    col_a = lax.broadcasted_iota(jnp.int32, (1, 2 * t), 1) < t
    q2 = q_ref[...]
    zero = jnp.zeros_like(q2)
    qs = jnp.concatenate([jnp.where(first, q2, zero), jnp.where(first, zero, q2)], axis=0)
    fq = jnp.concatenate([fq_ref[0:1, :], fq_ref[1:2, :]], axis=1)
    key = lax.broadcasted_iota(jnp.int32, (t, 2 * t), 0)
    qry = lax.broadcasted_iota(jnp.int32, (t, 2 * t), 1)
    causal = key <= jnp.where(col_a, qry, qry - t)
    acc_ref[...] = jnp.zeros_like(acc_ref)

    def tile(j, m, l, diagonal):
        start = pl.multiple_of(j * t, t)
        kt = k_ref[pl.ds(start, t), :]
        vt = v_ref[pl.ds(start, t), :]
        fk = fk_ref[pl.ds(start, t), :]
        s = (_dot_nt(kt, qs) + fq) - jnp.where(col_a, fk[:, 0:1], fk[:, 1:2])
        if diagonal:
            s = jnp.where(causal, s, NEG_INF)
        m_new = jnp.maximum(m, jnp.max(s, axis=0, keepdims=True))
        p = jnp.exp(s - m_new)
        alpha = jnp.exp(m - m_new)
        acc_ref[...] = alpha * acc_ref[...] + _dot_tn(vt, p.astype(BF16))
        return m_new, alpha * l + jnp.sum(p, axis=0, keepdims=True)

    m0 = jnp.full((1, 2 * t), -jnp.inf, F32)
    l0 = jnp.zeros((1, 2 * t), F32)
    m, l = lax.fori_loop(0, qi, lambda j, c: tile(j, c[0], c[1], False), (m0, l0))
    m, l = tile(qi, m, l, True)
    o_t = acc_ref[...] / l
    row_a = lax.broadcasted_iota(jnp.int32, (LANES, 1), 0) < FOX_HD
    o_ref[...] = jnp.where(row_a, o_t[:, :t], o_t[:, t:]).T.astype(BF16)


def _fox_prompt_call(q, k16, v16, f_rows, f_cols, tq):
    b, l, fd = q.shape
    npair = fd // LANES
    return pl.pallas_call(
        functools.partial(_fox_kernel, t=tq),
        grid=(b, npair, l // tq),
        in_specs=[
            pl.BlockSpec((None, tq, LANES), lambda bb, hp, i: (bb, i, hp)),
            pl.BlockSpec((None, l, LANES), lambda bb, hp, i: (bb, 0, hp)),
            pl.BlockSpec((None, l, LANES), lambda bb, hp, i: (bb, 0, hp)),
            pl.BlockSpec((None, None, None, 2, tq), lambda bb, hp, i: (bb, hp, i, 0, 0)),
            pl.BlockSpec((None, None, l, 2), lambda bb, hp, i: (bb, hp, 0, 0)),
        ],
        out_specs=pl.BlockSpec((None, tq, LANES), lambda bb, hp, i: (bb, i, hp)),
        out_shape=jax.ShapeDtypeStruct((b, l, fd), BF16),
        scratch_shapes=[pltpu.VMEM((LANES, 2 * tq), F32)],
        compiler_params=_cparams(3),
        name="fox_attention_prompt",
    )(q, k16, v16, f_rows, f_cols)


def _attn_sample_kernel(pt_ref, q_ref, kn_ref, vn_ref, lfn_ref, *refs, pps, page):
    ck = refs[0:pps]
    cv = refs[pps:2 * pps]
    clf = refs[2 * pps:3 * pps]
    o_ref = refs[3 * pps]
    m_ref, l_ref, acc_ref, carry_ref = refs[3 * pps + 1:]
    step = pl.program_id(1)
    nh, hd = FOX_HEADS, FOX_HD
    fd = nh * hd
    diag = (lax.broadcasted_iota(jnp.int32, (nh, fd), 1) // hd) == lax.broadcasted_iota(jnp.int32, (nh, fd), 0)
    q_bd = jnp.where(diag, jnp.broadcast_to(q_ref[...], (nh, fd)), 0.0)

    @pl.when(step == 0)
    def _():
        m_ref[...] = jnp.sum(q_bd * kn_ref[...], axis=-1, keepdims=True)
        l_ref[...] = jnp.ones_like(l_ref)
        acc_ref[...] = jnp.broadcast_to(vn_ref[...], (nh, fd))
        carry_ref[...] = lfn_ref[...]

    tt = lax.broadcasted_iota(jnp.int32, (page, 2 * page), 0)
    cc = lax.broadcasted_iota(jnp.int32, (page, 2 * page), 1)
    later = ((tt > cc) | (cc >= page)).astype(BF16)

    def split2(x):
        hi = x.astype(BF16)
        return jnp.concatenate([hi, (x - hi.astype(F32)).astype(BF16)], axis=0)

    q2 = split2(q_bd)
    lf_parts = [part for i in range(pps) for part in _split3(clf[i][...])]
    lfx_all = _dot(jnp.concatenate(lf_parts, axis=0), later)
    carry = carry_ref[...]
    scores = []
    for i in range(pps):
        base = 3 * nh * i
        lfx = (lfx_all[base:base + nh] + lfx_all[base + nh:base + 2 * nh]) + lfx_all[base + 2 * nh:base + 3 * nh]
        bias = lfx[:, :page] + carry
        carry = carry + lfx[:, page:page + 1]
        s2 = _dot(q2, ck[i][...].reshape(fd, page).astype(BF16))
        scores.append((s2[:nh] + s2[nh:]) + bias)
    carry_ref[...] = carry
    s = jnp.concatenate(scores, axis=1)
    m_old = m_ref[...]
    m_new = jnp.maximum(m_old, jnp.max(s, axis=-1, keepdims=True))
    p = jnp.exp(s - m_new)
    alpha = jnp.exp(m_old - m_new)
    m_ref[...] = m_new
    l_ref[...] = alpha * l_ref[...] + jnp.sum(p, axis=-1, keepdims=True)
    p2 = split2(p)
    pv = _dot_nt(p2[:, 0:page], cv[0][...].reshape(fd, page).astype(BF16))
    for i in range(1, pps):
        pv = pv + _dot_nt(p2[:, i * page:(i + 1) * page], cv[i][...].reshape(fd, page).astype(BF16))
    acc_ref[...] = alpha * acc_ref[...] + (pv[:nh] + pv[nh:])

    @pl.when(step == pl.num_programs(1) - 1)
    def _():
        o = acc_ref[...] / l_ref[...]
        o_ref[...] = jnp.sum(jnp.where(diag, o, 0.0), axis=0, keepdims=True)


def _attn_sample_call(page_table, q, k_new, v_new, lf_new, cache_kt, cache_vt, cache_lft, pps):
    nb, n_pages = page_table.shape
    n_pool, nh, hd, page = cache_kt.shape
    fd = nh * hd
    steps = n_pages // pps

    def page_map(i, nd):
        return lambda b, s, pt: (pt[b, n_pages - 1 - (s * pps + i)],) + (0,) * nd

    row = lambda r, w: pl.BlockSpec((None, r, w), lambda b, s, pt: (b, 0, 0))
    in_specs = [row(1, fd), row(1, fd), row(1, fd), row(nh, 1)]
    in_specs += [pl.BlockSpec((None, nh, hd, page), page_map(i, 3)) for i in range(pps)]
    in_specs += [pl.BlockSpec((None, nh, hd, page), page_map(i, 3)) for i in range(pps)]
    in_specs += [pl.BlockSpec((None, nh, page), page_map(i, 2)) for i in range(pps)]
    grid_spec = pltpu.PrefetchScalarGridSpec(
        num_scalar_prefetch=1,
        grid=(nb, steps),
        in_specs=in_specs,
        out_specs=row(1, fd),
        scratch_shapes=[pltpu.VMEM((nh, 1), F32), pltpu.VMEM((nh, 1), F32), pltpu.VMEM((nh, fd), F32),
                        pltpu.VMEM((nh, 1), F32)],
    )
    return pl.pallas_call(
        functools.partial(_attn_sample_kernel, pps=pps, page=page),
        grid_spec=grid_spec,
        out_shape=jax.ShapeDtypeStruct((nb, 1, fd), F32),
        compiler_params=_cparams(2),
        name="fox_attention_sample",
    )(page_table, q, k_new, v_new, lf_new, *([cache_kt] * pps), *([cache_vt] * pps), *([cache_lft] * pps))


def _rope_tables(pos):
    half = RET_DK // 2
    inv = ROPE_BASE ** (-jnp.arange(half, dtype=F32) / half)
    ang = pos.astype(F32)[:, None] * inv[None, :]
    return jnp.cos(ang), jnp.sin(ang)


def _pick_tile(n, pref):
    t = min(n, pref)
    while n % t:
        t //= 2
    return t


def kernel(x_prompt, x_sample, c_prompt, c_sample, state_ret, cache_k, cache_v, cache_logf, page_table, w_mod, b_mod, g_mix, g_ffn, w_ret_in, w_ret_out, g_kv, w_kvmod, b_kvmod, w_kv, w_f, b_f, w_fq, w_fo, w_rg, w_re, w_e_gate, w_e_up, w_e_down, g_final):
    b, l, d = x_prompt.shape
    nb = x_sample.shape[0]
    n_pool, page = cache_k.shape[0], cache_k.shape[1]
    n_pages = page_table.shape[1]
    past_len = n_pages * page
    fd = FOX_HEADS * FOX_HD

    c_all = jnp.concatenate([c_sample, c_prompt], axis=0)
    mod_all = _mod_call(c_all, w_mod, b_mod, 1536)
    kvmod_all = _mod_call(c_all, w_kvmod[None], b_kvmod[None], 1024)

    w_f_pad = jnp.pad(w_f, ((0, 0), (0, LANES - FOX_HEADS)))
    weights32 = dict(w_in=w_ret_in, w_out=w_ret_out[0], w_kv=w_kv, w_fq=w_fq[0], w_fo=w_fo[0], w_f=w_f_pad,
                     wg=w_e_gate, wu=w_e_up, wd=w_e_down)
    weights16 = {name: w.astype(BF16) for name, w in weights32.items()}
    b_f_row = jnp.pad(b_f, (0, LANES - FOX_HEADS)).reshape(1, LANES)
    w_router = jnp.concatenate([jnp.pad(w_rg, ((0, 0), (0, 0), (0, LANES - N_GROUPS))),
                                jnp.pad(w_re, ((0, 0), (0, 0), (0, LANES - N_EXPERTS)))], axis=-1)
    g_mix2 = g_mix.reshape(-1, 1, d)
    g_ffn2 = g_ffn.reshape(-1, 1, d)
    g_kv2 = g_kv.reshape(1, d)
    g_fin2 = g_final.reshape(1, d)

    def trunk(x, mod, kvmod, pos, tm, tm_in, tm_moe, mixer0, mixer1, precise):
        w = weights32 if precise else weights16
        cos, sin = _rope_tables(pos)
        proj = _retin_call(x, mod, g_mix2[0], w["w_in"], cos, sin, tm_in, precise)
        o, s_new = mixer0(proj)
        def moe(h, gates, layer, x_res, final):
            if precise:
                return _moe_call(h, gates, w["wg"], w["wu"], w["wd"], layer, x_res, mod, g_fin2, tm_moe, final, True)
            return _moe_sorted_call(h, gates, w["wg"], w["wu"], w["wd"], layer, x_res, mod, g_fin2, tm_moe, final)

        x1, h2, gates = _mixout_call(o, w["w_out"], x, mod, 0, g_ffn2[0], w_router[0], tm, precise)
        x2 = moe(h2, gates, 0, x1, False)
        k32, v32, k16, v16, lf, qf = _kvq_call(x2, kvmod, mod, g_kv2, w["w_kv"], w["w_f"], b_f_row, g_mix2[1],
                                               w["w_fq"], tm, precise)
        o1 = mixer1(qf, k32, v32, k16, v16, lf)
        x3, h4, gates1 = _mixout_call(o1, w["w_fo"], x2, mod, 1, g_ffn2[1], w_router[1], tm, precise)
        y = moe(h4, gates1, 1, x3, True)
        return y, s_new, k32, v32, lf

    tm_p = _pick_tile(l, 512)
    chunk = _pick_tile(l, 256)
    tq = _pick_tile(l, 512)

    def ret_prompt(proj):
        return _ret_prompt_call(proj, chunk)

    def fox_prompt(qf, k32, v32, k16, v16, lf):
        f_t = _cumsum_call(jnp.swapaxes(lf, 1, 2))
        f_pairs = f_t.reshape(b, FOX_HEADS // 2, 2, l)
        f_cols = jnp.swapaxes(f_pairs, 2, 3)
        f_rows = jnp.swapaxes(f_pairs.reshape(b, FOX_HEADS // 2, 2, l // tq, tq), 2, 3)
        return _fox_prompt_call(qf, k16, v16, f_rows, f_cols, tq)

    mod_p = _Mod(mod_all, False, nb, tm_p)
    kvmod_p = _Mod(kvmod_all, False, nb, tm_p)
    y_p, s_p, k_p, v_p, lf_p = trunk(x_prompt, mod_p, kvmod_p, jnp.arange(l), tm_p, _pick_tile(l, 1024),
                                     _pick_tile(l, 1024), ret_prompt, fox_prompt, False)

    def ret_sample(proj):
        o, s_new = _ret_sample_call(proj[0], state_ret[0])
        return o.reshape(1, nb, -1), s_new

    def fox_sample(qf, k32, v32, k16, v16, lf):
        rows = lambda a: a.reshape(nb, 1, fd)
        o = _attn_sample_call(page_table, rows(qf), rows(k32), rows(v32), lf.reshape(nb, FOX_HEADS, 1),
                              jnp.transpose(cache_k, (0, 2, 3, 1)), jnp.transpose(cache_v, (0, 2, 3, 1)),
                              jnp.swapaxes(cache_logf, 1, 2), _pick_tile(n_pages, 8))
        return o.reshape(1, nb, fd)

    mod_s = _Mod(mod_all, True, 0, nb)
    kvmod_s = _Mod(kvmod_all, True, 0, nb)
    y_s, s_s, k_s, v_s, lf_s = trunk(x_sample.reshape(1, nb, d), mod_s, kvmod_s,
                                     jnp.full((nb,), past_len, jnp.int32), nb, nb, nb, ret_sample, fox_sample, True)

    return (y_p, y_s.reshape(nb, 1, d), s_p[None], s_s[None],
            k_p.reshape(b, l, FOX_HEADS, FOX_HD), v_p.reshape(b, l, FOX_HEADS, FOX_HD), lf_p,
            k_s.reshape(nb, 1, FOX_HEADS, FOX_HD), v_s.reshape(nb, 1, FOX_HEADS, FOX_HD),
            lf_s.reshape(nb, 1, FOX_HEADS))
```

```python
import functools
import math

import jax
import jax.numpy as jnp
from jax import lax
from jax.experimental import pallas as pl
from jax.experimental.pallas import tpu as pltpu

D_MODEL = 1024
RET_HEADS = 4
RET_DK = 256
RET_DV = 512
ROPE_BASE = 10000.0
FOX_HEADS = 16
FOX_HD = 64
N_GROUPS = 4
EXPERTS_PER_GROUP = 8
N_EXPERTS = 32
D_EXPERT = 256
EPS = 1e-6
NEG_INF = -1e30

LANES = 128
VMEM_LIMIT = 56 * 1024 * 1024
GROUP_ID_LANE = 64
MOE_CHUNK = 128
MOE_WINDOW = 3
MOE_EXPERTS_PER_STEP = 4
F32 = jnp.float32
BF16 = jnp.bfloat16


def _cparams(n_axes):
    return pltpu.CompilerParams(dimension_semantics=("arbitrary",) * n_axes,
                                vmem_limit_bytes=VMEM_LIMIT)


def _silu(x):
    return x * jax.nn.sigmoid(x)


def _dot(a, b):
    return jnp.dot(a, b, preferred_element_type=F32)


def _dot_nt(a, b):
    return lax.dot_general(a, b, (((1,), (1,)), ((), ())), preferred_element_type=F32)


def _dot_tn(a, b):
    return lax.dot_general(a, b, (((0,), (0,)), ((), ())), preferred_element_type=F32)


def _mm(a, w, precise):
    if precise:
        return jnp.dot(a, w, preferred_element_type=F32, precision=lax.Precision.HIGHEST)
    return _dot(a.astype(BF16), w)


def _split3(x):
    hi = x.astype(BF16)
    r1 = x - hi.astype(F32)
    mid = r1.astype(BF16)
    lo = (r1 - mid.astype(F32)).astype(BF16)
    return hi, mid, lo


def _rms_mod(x, g, shift, scale):
    y = x * lax.rsqrt(jnp.mean(x * x, axis=-1, keepdims=True) + EPS)
    return (y * g) * (1.0 + scale) + shift


def _mod_kernel(c_ref, w_ref, b_ref, o_ref):
    o_ref[...] = _mm(_silu(c_ref[...]), w_ref[...], True) + b_ref[...]


def _mod_call(c, w, b, tn):
    ns, k, n = w.shape
    m = c.shape[0]
    return pl.pallas_call(
        _mod_kernel,
        grid=(ns, n // tn),
        in_specs=[
            pl.BlockSpec((m, k), lambda s, j: (0, 0)),
            pl.BlockSpec((None, k, tn), lambda s, j: (s, 0, j)),
            pl.BlockSpec((None, 1, tn), lambda s, j: (s, 0, j)),
        ],
        out_specs=pl.BlockSpec((None, m, tn), lambda s, j: (s, 0, j)),
        out_shape=jax.ShapeDtypeStruct((ns, m, n), F32),
        compiler_params=_cparams(2),
        name="adaln_mod",
    )(c, w, b.reshape(ns, 1, n))


class _Mod:
    def __init__(self, arr, per_token, row0, tm):
        self.per_token = per_token
        self.row0 = row0
        self.tm = tm
        self.arr = arr if per_token else arr.reshape(arr.shape[0], arr.shape[1], 1, arr.shape[2])

    def spec(self, stack, col):
        if self.per_token:
            assert self.row0 % self.tm == 0
            r0 = self.row0 // self.tm
            return pl.BlockSpec((None, self.tm, D_MODEL), lambda b, i: (stack, r0 + i, col))
        row0 = self.row0
        return pl.BlockSpec((None, None, 1, D_MODEL), lambda b, i: (stack, row0 + b, 0, col))


def _tok_spec(tm, width):
    return pl.BlockSpec((None, tm, width), lambda b, i: (b, i, 0))


def _const_spec(shape):
    nd = len(shape)
    return pl.BlockSpec(shape, lambda b, i: (0,) * nd)


def _retin_kernel(x_ref, sh_ref, sc_ref, g_ref, w_ref, cos_ref, sin_ref, o_ref, h_ref, *, precise):
    j = pl.program_id(2)

    @pl.when(j == 0)
    def _():
        h_ref[...] = _rms_mod(x_ref[...], g_ref[...], sh_ref[...], sc_ref[...]).astype(h_ref.dtype)

    p = _mm(h_ref[...], w_ref[...], precise)

    @pl.when(j < 2)
    def _():
        scale = jnp.where(j == 0, 1.0, RET_DK ** -0.5)
        half = RET_DK // 2
        cos = cos_ref[...]
        sin = sin_ref[...]
        for hh in range(RET_HEADS):
            a = hh * RET_DK
            x1 = p[:, a:a + half]
            x2 = p[:, a + half:a + RET_DK]
            o_ref[:, a:a + half] = ((x1 * cos - x2 * sin) * scale).astype(o_ref.dtype)
            o_ref[:, a + half:a + RET_DK] = ((x1 * sin + x2 * cos) * scale).astype(o_ref.dtype)

    @pl.when(j >= 2)
    def _():
        o_ref[...] = p.astype(o_ref.dtype)


def _retin_call(x, mod, g_mix, w_in, cos, sin, tm, precise):
    bx, lx, d = x.shape
    tn = RET_HEADS * RET_DK
    n = w_in.shape[-1]
    dt = F32 if precise else BF16
    spec3 = lambda sp: pl.BlockSpec(sp.block_shape, lambda b, i, j: sp.index_map(b, i))
    return pl.pallas_call(
        functools.partial(_retin_kernel, precise=precise),
        grid=(bx, lx // tm, n // tn),
        in_specs=[
            spec3(_tok_spec(tm, d)),
            spec3(mod.spec(0, 0)), spec3(mod.spec(0, 1)),
            spec3(_const_spec((1, d))),
            pl.BlockSpec((None, d, tn), lambda b, i, j: (0, 0, j)),
            pl.BlockSpec((tm, RET_DK // 2), lambda b, i, j: (i, 0)),
            pl.BlockSpec((tm, RET_DK // 2), lambda b, i, j: (i, 0)),
        ],
        out_specs=pl.BlockSpec((None, tm, tn), lambda b, i, j: (b, i, j)),
        out_shape=jax.ShapeDtypeStruct((bx, lx, n), dt),
        scratch_shapes=[pltpu.VMEM((tm, d), dt)],
        compiler_params=_cparams(3),
        name="ret_in_proj",
    )(x, mod.arr, mod.arr, g_mix, w_in, cos, sin)


def _log_gamma(h):
    return math.log(1.0 - 2.0 ** (-5.0 - h))


def _ret_prompt_kernel(q_ref, k_ref, v_ref, g_ref, o_ref, s_ref, *, chunk):
    @pl.when(pl.program_id(1) == 0)
    def _():
        s_ref[...] = jnp.zeros_like(s_ref)

    ti = lax.broadcasted_iota(jnp.int32, (chunk, chunk), 0)
    tj = lax.broadcasted_iota(jnp.int32, (chunk, chunk), 1)
    rel = (ti - tj).astype(F32)
    t = lax.broadcasted_iota(jnp.int32, (chunk, 1), 0).astype(F32)
    for h in range(RET_HEADS):
        lg = _log_gamma(h)
        decay = jnp.where(rel >= 0, jnp.exp(lg * jnp.maximum(rel, 0.0)), 0.0)
        qh = q_ref[:, h * RET_DK:(h + 1) * RET_DK]
        kh = k_ref[:, h * RET_DK:(h + 1) * RET_DK]
        vh = v_ref[:, h * RET_DV:(h + 1) * RET_DV]
        a = _dot_nt(qh, kh) * decay
        inner = _dot(a.astype(BF16), vh)
        s_old = s_ref[h]
        cross = _dot(qh, s_old.astype(BF16)) * jnp.exp((t + 1.0) * lg)
        o = inner + cross
        kd = (kh.astype(F32) * jnp.exp((chunk - 1.0 - t) * lg)).astype(BF16)
        s_ref[h] = math.exp(chunk * lg) * s_old + _dot_tn(kd, vh)
        o = o * lax.rsqrt(jnp.mean(o * o, axis=-1, keepdims=True) + EPS)
        gh = g_ref[:, h * RET_DV:(h + 1) * RET_DV].astype(F32)
        o_ref[:, h * RET_DV:(h + 1) * RET_DV] = (o * _silu(gh)).astype(BF16)


def _ret_prompt_call(proj, chunk):
    b, l, _ = proj.shape
    qd = RET_HEADS * RET_DK
    vd = RET_HEADS * RET_DV
    col = lambda w, j: pl.BlockSpec((None, chunk, w), lambda bb, c: (bb, c, j))
    return pl.pallas_call(
        functools.partial(_ret_prompt_kernel, chunk=chunk),
        grid=(b, l // chunk),
        in_specs=[col(qd, 0), col(qd, 1), col(vd, 1), col(vd, 2)],
        out_specs=[_tok_spec(chunk, vd),
                   pl.BlockSpec((None, RET_HEADS, RET_DK, RET_DV), lambda bb, c: (bb, 0, 0, 0))],
        out_shape=[jax.ShapeDtypeStruct((b, l, vd), BF16),
                   jax.ShapeDtypeStruct((b, RET_HEADS, RET_DK, RET_DV), F32)],
        compiler_params=_cparams(2),
        name="retention_prompt",
    )(proj, proj, proj, proj)


def _ret_sample_kernel(q_ref, k_ref, v_ref, g_ref, s_ref, o_ref, sn_ref):
    b = pl.program_id(0)
    h = pl.program_id(1)
    nb = q_ref.shape[0]
    gamma = 1.0 - 1.0 / (jnp.zeros((1, 1), F32) + (jnp.int32(32) << h).astype(F32))
    rowsel = lax.broadcasted_iota(jnp.int32, (nb, 1), 0) == b
    q = jnp.where(rowsel, q_ref[...], 0.0)
    k = jnp.where(rowsel, k_ref[...], 0.0)
    v = v_ref[...]
    vb = jnp.sum(jnp.where(rowsel, v, 0.0), axis=0, keepdims=True)
    gb = jnp.sum(jnp.where(rowsel, g_ref[...], 0.0), axis=0, keepdims=True)
    s_old = s_ref[...]
    kv = lax.dot_general(k, v, (((0,), (0,)), ((), ())), preferred_element_type=F32,
                         precision=lax.Precision.HIGHEST)
    sn_ref[...] = gamma * s_old + kv
    qk = jnp.sum(jnp.sum(q * k, axis=1, keepdims=True), axis=0, keepdims=True)
    cross = jnp.sum(_mm(q, s_old, True), axis=0, keepdims=True) * gamma
    o = qk * vb + cross
    o = o * lax.rsqrt(jnp.mean(o * o, axis=-1, keepdims=True) + EPS)
    o_ref[...] = o * _silu(gb)


def _ret_sample_call(proj, s0):
    nb = proj.shape[0]
    k0 = RET_HEADS
    v0 = 2 * RET_HEADS * RET_DK // RET_DV
    g0 = v0 + RET_HEADS
    return pl.pallas_call(
        _ret_sample_kernel,
        grid=(nb, RET_HEADS),
        in_specs=[
            pl.BlockSpec((nb, RET_DK), lambda b, h: (0, h)),
            pl.BlockSpec((nb, RET_DK), lambda b, h: (0, k0 + h)),
            pl.BlockSpec((nb, RET_DV), lambda b, h: (0, v0 + h)),
            pl.BlockSpec((nb, RET_DV), lambda b, h: (0, g0 + h)),
            pl.BlockSpec((None, None, RET_DK, RET_DV), lambda b, h: (b, h, 0, 0)),
        ],
        out_specs=[pl.BlockSpec((None, 1, RET_DV), lambda b, h: (b, 0, h)),
                   pl.BlockSpec((None, None, RET_DK, RET_DV), lambda b, h: (b, h, 0, 0))],
        out_shape=[jax.ShapeDtypeStruct((nb, 1, RET_HEADS * RET_DV), F32),
                   jax.ShapeDtypeStruct((nb, RET_HEADS, RET_DK, RET_DV), F32)],
        compiler_params=_cparams(2),
        name="retention_sample",
    )(proj, proj, proj, proj, s0)


def _route(logits):
    tm = logits.shape[0]
    lane = lax.broadcasted_iota(jnp.int32, (tm, LANES), 1).astype(F32)
    ninf = -jnp.inf
    lg = jnp.where((lane >= N_EXPERTS) & (lane < N_EXPERTS + N_GROUPS), logits, ninf)
    mx = jnp.max(lg, axis=-1, keepdims=True)
    p_group = 1.0 / jnp.sum(jnp.exp(lg - mx), axis=-1, keepdims=True)
    gi = jnp.min(jnp.where(lg == mx, lane, float(LANES)), axis=-1, keepdims=True) - N_EXPERTS
    lo = gi * EXPERTS_PER_GROUP
    les = jnp.where((lane >= lo) & (lane < lo + EXPERTS_PER_GROUP), logits, ninf)
    v1 = jnp.max(les, axis=-1, keepdims=True)
    i1 = jnp.min(jnp.where(les == v1, lane, float(LANES)), axis=-1, keepdims=True)
    les2 = jnp.where(lane == i1, ninf, les)
    v2 = jnp.max(les2, axis=-1, keepdims=True)
    i2 = jnp.min(jnp.where(les2 == v2, lane, float(LANES)), axis=-1, keepdims=True)
    e2 = jnp.exp(v2 - v1)
    w1 = 1.0 / (1.0 + e2)
    w2 = e2 / (1.0 + e2)
    gates = jnp.where(lane == i1, w1 * p_group, 0.0) + jnp.where(lane == i2, w2 * p_group, 0.0)
    return gates + jnp.where(lane == GROUP_ID_LANE, gi, 0.0)


def _mixout_kernel(o_ref, w_ref, x_ref, g1_ref, sh_ref, sc_ref, gf_ref, wr_ref, x1_ref, h2_ref, gates_ref, *,
                   precise):
    y = _mm(o_ref[...], w_ref[...], precise)
    x1 = x_ref[...] + g1_ref[...] * y
    x1_ref[...] = x1
    h2 = _rms_mod(x1, gf_ref[...], sh_ref[...], sc_ref[...])
    h2_ref[...] = h2.astype(h2_ref.dtype)
    if precise:
        logits = _mm(h2, wr_ref[0], True)
    else:
        h_hi = h2.astype(BF16)
        h_lo = (h2 - h_hi.astype(F32)).astype(BF16)
        logits = _dot(h_hi, wr_ref[0]) + (_dot(h_lo, wr_ref[0]) + _dot(h_hi, wr_ref[1]))
    gates_ref[...] = _route(logits)


def _mixout_call(o, w_out, x, mod, layer, g_ffn, w_router, tm, precise):
    bx, lx, d = x.shape
    kd = o.shape[-1]
    return pl.pallas_call(
        functools.partial(_mixout_kernel, precise=precise),
        grid=(bx, lx // tm),
        in_specs=[
            _tok_spec(tm, kd),
            _const_spec((kd, d)),
            _tok_spec(tm, d),
            mod.spec(layer, 2), mod.spec(layer, 3), mod.spec(layer, 4),
            _const_spec((1, d)),
            _const_spec((2, d, LANES)),
        ],
        out_specs=[_tok_spec(tm, d), _tok_spec(tm, d), _tok_spec(tm, LANES)],
        out_shape=[jax.ShapeDtypeStruct((bx, lx, d), F32),
                   jax.ShapeDtypeStruct((bx, lx, d), F32 if precise else BF16),
                   jax.ShapeDtypeStruct((bx, lx, LANES), F32)],
        compiler_params=_cparams(2),
        name="mix_out_router",
    )(o, w_out, x, mod.arr, mod.arr, mod.arr, g_ffn, w_router)


def _moe_kernel(h_ref, gates_ref, wg_ref, wu_ref, wd_ref, x1_ref, g2_ref, gfin_ref, out_ref, acc_ref, *, final,
                precise):
    e = pl.program_id(2)

    @pl.when(e == 0)
    def _():
        acc_ref[...] = jnp.zeros_like(acc_ref)

    h = h_ref[...]
    a = _mm(h, wg_ref[...], precise)
    u = _mm(h, wu_ref[...], precise)
    lane = lax.broadcasted_iota(jnp.int32, gates_ref.shape, 1)
    ge = jnp.sum(jnp.where(lane == e, gates_ref[...], 0.0), axis=-1, keepdims=True)
    act = (_silu(a) * u) * ge
    acc_ref[...] += _mm(act, wd_ref[...], precise)

    @pl.when(e == pl.num_programs(2) - 1)
    def _():
        x2 = x1_ref[...] + g2_ref[...] * acc_ref[...]
        if final:
            x2 = (x2 * lax.rsqrt(jnp.mean(x2 * x2, axis=-1, keepdims=True) + EPS)) * gfin_ref[...]
        out_ref[...] = x2


def _moe_call(h2, gates, wg, wu, wd, layer, x1, mod, g_final, tm, final, precise):
    bx, lx, d = x1.shape
    tok3 = lambda w: pl.BlockSpec((None, tm, w), lambda b, i, e: (b, i, 0))
    g2_spec2 = mod.spec(layer, 5)
    g2_spec = pl.BlockSpec(g2_spec2.block_shape, lambda b, i, e: g2_spec2.index_map(b, i))
    return pl.pallas_call(
        functools.partial(_moe_kernel, final=final, precise=precise),
        grid=(bx, lx // tm, N_EXPERTS),
        in_specs=[
            tok3(d), tok3(LANES),
            pl.BlockSpec((None, d, D_EXPERT), lambda b, i, e: (layer, 0, e)),
            pl.BlockSpec((None, d, D_EXPERT), lambda b, i, e: (layer, 0, e)),
            pl.BlockSpec((None, D_EXPERT, d), lambda b, i, e: (layer, e, 0)),
            tok3(d),
            g2_spec,
            pl.BlockSpec((1, d), lambda b, i, e: (0, 0)),
        ],
        out_specs=tok3(d),
        out_shape=jax.ShapeDtypeStruct((bx, lx, d), F32),
        scratch_shapes=[pltpu.VMEM((tm, d), F32)],
        compiler_params=_cparams(3),
        name="hier_moe",
    )(h2, gates, wg, wu, wd, x1, mod.arr, g_final)


def _moe_plan_kernel(gates_ref, pos_ref, gs_ref, flags_ref, *, chunk):
    g = gates_ref[...]
    tm = g.shape[0]
    lane = lax.broadcasted_iota(jnp.int32, (tm, LANES), 1)
    gid = jnp.sum(jnp.where(lane == GROUP_ID_LANE, g, 0.0), axis=-1, keepdims=True)
    onehot = jnp.where((lane.astype(F32) == gid) & (lane < N_GROUPS), 1.0, 0.0)
    r = lax.broadcasted_iota(jnp.int32, (tm, tm), 0)
    c = lax.broadcasted_iota(jnp.int32, (tm, tm), 1)
    before = _dot((c < r).astype(BF16), onehot.astype(BF16))
    rank = jnp.sum(onehot * before, axis=-1, keepdims=True)
    count = jnp.broadcast_to(jnp.sum(onehot, axis=0, keepdims=True), (8, LANES))
    lower = (lax.broadcasted_iota(jnp.int32, (LANES, LANES), 0)
             < lax.broadcasted_iota(jnp.int32, (LANES, LANES), 1)).astype(BF16)
    hi, mid, lo = _split3(count)
    start = (_dot(hi, lower) + _dot(mid, lower)) + _dot(lo, lower)
    pos = jnp.sum(onehot * start[0:1], axis=-1, keepdims=True) + rank
    pos_ref[...] = jnp.broadcast_to(pos, (tm, LANES))
    place = (pos == c.astype(F32)).astype(BF16)
    hi, mid, lo = _split3(jnp.where(lane < N_EXPERTS, g, 0.0))
    gs = (_dot_tn(place, hi) + _dot_tn(place, mid)) + _dot_tn(place, lo)
    gs_ref[...] = gs
    flags_ref[...] = jnp.zeros_like(flags_ref)
    for k in range(tm // chunk):
        used = jnp.max(jnp.where(gs[k * chunk:(k + 1) * chunk] != 0.0, 1.0, 0.0), axis=0, keepdims=True)
        flags_ref[k:k + 1, :] = used.astype(jnp.int32)


def _moe_plan_call(gates, tm, chunk):
    bx, lx, _ = gates.shape
    n_chunks = tm // chunk
    rows = -(-n_chunks // 8) * 8
    tok = lambda: pl.BlockSpec((None, tm, LANES), lambda b, i: (b, i, 0))
    return pl.pallas_call(
        functools.partial(_moe_plan_kernel, chunk=chunk),
        grid=(bx, lx // tm),
        in_specs=[tok()],
        out_specs=[tok(), tok(), pl.BlockSpec((None, None, rows, LANES), lambda b, i: (b, i, 0, 0))],
        out_shape=[jax.ShapeDtypeStruct((bx, lx, LANES), F32), jax.ShapeDtypeStruct((bx, lx, LANES), F32),
                   jax.ShapeDtypeStruct((bx, lx // tm, rows, LANES), jnp.int32)],
        compiler_params=_cparams(2),
        name="moe_plan",
    )(gates)


def _moe_sorted_kernel(first_ref, nwin_ref, h_ref, pos_ref, gs_ref, wg_ref, wu_ref, wd_ref, x1_ref, g2_ref, gfin_ref,
                       out_ref, place_ref, xs_ref, gsp_ref, acc_ref, *, final, chunk, window, eps):
    b = pl.program_id(0)
    i = pl.program_id(1)
    step = pl.program_id(2)
    tm = h_ref.shape[0]
    tile = b * pl.num_programs(1) + i

    @pl.when(step == 0)
    def _():
        dest = lax.broadcasted_iota(jnp.int32, (tm, tm), 1).astype(F32)
        place = (pos_ref[:, 0:1] == dest).astype(BF16)
        place_ref[...] = place
        xs_ref[...] = jnp.zeros_like(xs_ref)
        xs_ref[0:tm, :] = _dot_tn(place, h_ref[...]).astype(BF16)
        gsp_ref[...] = jnp.zeros_like(gsp_ref)
        gsp_ref[0:tm, :] = gs_ref[...]
        acc_ref[...] = jnp.zeros_like(acc_ref)

    wrows = window * chunk
    lane = lax.broadcasted_iota(jnp.int32, (wrows, LANES), 1)

    for j in range(eps):
        e = step * eps + j
        cols = slice(j * D_EXPERT, (j + 1) * D_EXPERT)

        def one_window(w, carry, e=e, cols=cols):
            rows = pl.ds(pl.multiple_of((first_ref[tile, e] + w * window) * chunk, chunk), wrows)
            x = xs_ref[rows, :]
            ge = jnp.sum(jnp.where(lane == e, gsp_ref[rows, :], 0.0), axis=-1, keepdims=True)
            act = (_silu(_dot(x, wg_ref[:, cols])) * _dot(x, wu_ref[:, cols])) * ge
            acc_ref[rows, :] += _dot(act.astype(BF16), wd_ref[cols, :])
            return carry

        lax.fori_loop(0, nwin_ref[tile, e], one_window, 0)

    @pl.when(step == pl.num_programs(2) - 1)
    def _():
        acc = acc_ref[0:tm, :]
        hi = acc.astype(BF16)
        lo = (acc - hi.astype(F32)).astype(BF16)
        place = place_ref[...]
        x2 = x1_ref[...] + g2_ref[...] * (_dot(place, hi) + _dot(place, lo))
        if final:
            x2 = (x2 * lax.rsqrt(jnp.mean(x2 * x2, axis=-1, keepdims=True) + EPS)) * gfin_ref[...]
        out_ref[...] = x2


def _moe_sorted_call(h2, gates, wg, wu, wd, layer, x1, mod, g_final, tm, final):
    bx, lx, d = x1.shape
    chunk = min(tm, MOE_CHUNK)
    n_chunks = tm // chunk
    window = min(n_chunks, MOE_WINDOW)
    pos, gs, flags = _moe_plan_call(gates, tm, chunk)
    used = flags[:, :, :n_chunks, :N_EXPERTS].reshape(-1, n_chunks, N_EXPERTS) != 0
    idx = jnp.arange(n_chunks, dtype=jnp.int32)[None, :, None]
    first = jnp.min(jnp.where(used, idx, n_chunks), axis=1)
    last = jnp.max(jnp.where(used, idx, -1), axis=1)
    nwin = jnp.where(last >= 0, (last - first) // window + 1, 0).astype(jnp.int32)
    first = jnp.where(last >= 0, first, 0).astype(jnp.int32)
    pad_rows = (window - 1) * chunk
    tok3 = lambda w: pl.BlockSpec((None, tm, w), lambda b, i, e, f, n: (b, i, 0))
    g2_spec2 = mod.spec(layer, 5)
    g2_spec = pl.BlockSpec(g2_spec2.block_shape, lambda b, i, e, f, n: g2_spec2.index_map(b, i))
    eps = MOE_EXPERTS_PER_STEP
    grid_spec = pltpu.PrefetchScalarGridSpec(
        num_scalar_prefetch=2,
        grid=(bx, lx // tm, N_EXPERTS // eps),
        in_specs=[
            tok3(d), tok3(LANES), tok3(LANES),
            pl.BlockSpec((None, d, eps * D_EXPERT), lambda b, i, e, f, n: (layer, 0, e)),
            pl.BlockSpec((None, d, eps * D_EXPERT), lambda b, i, e, f, n: (layer, 0, e)),
            pl.BlockSpec((None, eps * D_EXPERT, d), lambda b, i, e, f, n: (layer, e, 0)),
            tok3(d),
            g2_spec,
            pl.BlockSpec((1, d), lambda b, i, e, f, n: (0, 0)),
        ],
        out_specs=tok3(d),
        scratch_shapes=[pltpu.VMEM((tm, tm), BF16), pltpu.VMEM((tm + pad_rows, d), BF16),
                        pltpu.VMEM((tm + pad_rows, LANES), F32), pltpu.VMEM((tm + pad_rows, d), F32)],
    )
    return pl.pallas_call(
        functools.partial(_moe_sorted_kernel, final=final, chunk=chunk, window=window, eps=eps),
        grid_spec=grid_spec,
        out_shape=jax.ShapeDtypeStruct((bx, lx, d), F32),
        compiler_params=_cparams(3),
        name="hier_moe_sorted",
    )(first, nwin, h2, pos, gs, wg, wu, wd, x1, mod.arr, g_final)


def _log_sigmoid(z):
    return -(jnp.maximum(-z, 0.0) + jnp.log1p(jnp.exp(-jnp.abs(z))))


def _kvq_kernel(x_ref, shk_ref, sck_ref, gkv_ref, wkv_ref, wf_ref, bf_ref, sh1_ref, sc1_ref, gmix_ref, wq_ref,
                k32_ref, v32_ref, k16_ref, v16_ref, lf_ref, q_ref, *, precise):
    x = x_ref[...]
    fd = FOX_HEADS * FOX_HD
    n = _rms_mod(x, gkv_ref[...], shk_ref[...], sck_ref[...])
    if not precise:
        n = n.astype(BF16)
    k = _mm(n, wkv_ref[:, 0:fd], precise)
    k32_ref[...] = k
    k16_ref[...] = k.astype(BF16)
    v = _mm(n, wkv_ref[:, fd:2 * fd], precise)
    v32_ref[...] = v
    v16_ref[...] = v.astype(BF16)
    z = _mm(n, wf_ref[...], precise) + bf_ref[...]
    lf_ref[...] = _log_sigmoid(z)[:, :FOX_HEADS]
    h = _rms_mod(x, gmix_ref[...], sh1_ref[...], sc1_ref[...])
    q_ref[...] = (_mm(h, wq_ref[...], precise) * (FOX_HD ** -0.5)).astype(q_ref.dtype)


def _kvq_call(x, kvmod, mod, g_kv, w_kv, w_f, b_f, g_mix, w_q, tm, precise):
    bx, lx, d = x.shape
    fd = FOX_HEADS * FOX_HD
    return pl.pallas_call(
        functools.partial(_kvq_kernel, precise=precise),
        grid=(bx, lx // tm),
        in_specs=[
            _tok_spec(tm, d),
            kvmod.spec(0, 0), kvmod.spec(0, 1),
            _const_spec((1, d)),
            _const_spec((d, 2 * fd)),
            _const_spec((d, LANES)),
            _const_spec((1, LANES)),
            mod.spec(1, 0), mod.spec(1, 1),
            _const_spec((1, d)),
            _const_spec((d, fd)),
        ],
        out_specs=[_tok_spec(tm, fd), _tok_spec(tm, fd), _tok_spec(tm, fd), _tok_spec(tm, fd),
                   _tok_spec(tm, FOX_HEADS), _tok_spec(tm, fd)],
        out_shape=[jax.ShapeDtypeStruct((bx, lx, fd), F32), jax.ShapeDtypeStruct((bx, lx, fd), F32),
                   jax.ShapeDtypeStruct((bx, lx, fd), BF16), jax.ShapeDtypeStruct((bx, lx, fd), BF16),
                   jax.ShapeDtypeStruct((bx, lx, FOX_HEADS), F32),
                   jax.ShapeDtypeStruct((bx, lx, fd), F32 if precise else BF16)],
        compiler_params=_cparams(2),
        name="kv_q_proj",
    )(x, kvmod.arr, kvmod.arr, g_kv, w_kv, w_f, b_f, mod.arr, mod.arr, g_mix, w_q)


def _cumsum_kernel(x_ref, o_ref):
    nh, l = x_ref.shape
    r = lax.broadcasted_iota(jnp.int32, (LANES, LANES), 0)
    c = lax.broadcasted_iota(jnp.int32, (LANES, LANES), 1)
    upper = (r <= c).astype(BF16)
    carry = jnp.zeros((nh, 1), F32)
    for blk in range(l // LANES):
        hi, mid, lo = _split3(x_ref[:, blk * LANES:(blk + 1) * LANES])
        cs = (_dot(hi, upper) + _dot(mid, upper)) + _dot(lo, upper) + carry
        o_ref[:, blk * LANES:(blk + 1) * LANES] = cs
        carry = cs[:, LANES - 1:LANES]


def _cumsum_call(x):
    b, nh, l = x.shape
    return pl.pallas_call(
        _cumsum_kernel,
        grid=(b,),
        in_specs=[pl.BlockSpec((None, nh, l), lambda i: (i, 0, 0))],
        out_specs=pl.BlockSpec((None, nh, l), lambda i: (i, 0, 0)),
        out_shape=jax.ShapeDtypeStruct((b, nh, l), F32),
        compiler_params=_cparams(1),
        name="logf_cumsum",
    )(x)


def _fox_kernel(q_ref, k_ref, v_ref, fq_ref, fk_ref, o_ref, acc_ref, *, t):
    qi = pl.program_id(2)
    first = lax.broadcasted_iota(jnp.int32, (1, LANES), 1) < FOX_HD
    col_a = lax.broadcasted_iota(jnp.int32, (1, 2 * t), 1) < t
    q2 = q_ref[...]
    zero = jnp.zeros_like(q2)
    qs = jnp.concatenate([jnp.where(first, q2, zero), jnp.where(first, zero, q2)], axis=0)
    fq = jnp.concatenate([fq_ref[0:1, :], fq_ref[1:2, :]], axis=1)
    key = lax.broadcasted_iota(jnp.int32, (t, 2 * t), 0)
    qry = lax.broadcasted_iota(jnp.int32, (t, 2 * t), 1)
    causal = key <= jnp.where(col_a, qry, qry - t)
    acc_ref[...] = jnp.zeros_like(acc_ref)

    def tile(j, m, l, diagonal):
        start = pl.multiple_of(j * t, t)
        kt = k_ref[pl.ds(start, t), :]
        vt = v_ref[pl.ds(start, t), :]
        fk = fk_ref[pl.ds(start, t), :]
        s = (_dot_nt(kt, qs) + fq) - jnp.where(col_a, fk[:, 0:1], fk[:, 1:2])
        if diagonal:
            s = jnp.where(causal, s, NEG_INF)
        m_new = jnp.maximum(m, jnp.max(s, axis=0, keepdims=True))
        p = jnp.exp(s - m_new)
        alpha = jnp.exp(m - m_new)
        acc_ref[...] = alpha * acc_ref[...] + _dot_tn(vt, p.astype(BF16))
        return m_new, alpha * l + jnp.sum(p, axis=0, keepdims=True)

    m0 = jnp.full((1, 2 * t), -jnp.inf, F32)
    l0 = jnp.zeros((1, 2 * t), F32)
    m, l = lax.fori_loop(0, qi, lambda j, c: tile(j, c[0], c[1], False), (m0, l0))
    m, l = tile(qi, m, l, True)
    o_t = acc_ref[...] / l
    row_a = lax.broadcasted_iota(jnp.int32, (LANES, 1), 0) < FOX_HD
    o_ref[...] = jnp.where(row_a, o_t[:, :t], o_t[:, t:]).T.astype(BF16)


def _fox_prompt_call(q, k16, v16, f_rows, f_cols, tq):
    b, l, fd = q.shape
    npair = fd // LANES
    return pl.pallas_call(
        functools.partial(_fox_kernel, t=tq),
        grid=(b, npair, l // tq),
        in_specs=[
            pl.BlockSpec((None, tq, LANES), lambda bb, hp, i: (bb, i, hp)),
            pl.BlockSpec((None, l, LANES), lambda bb, hp, i: (bb, 0, hp)),
            pl.BlockSpec((None, l, LANES), lambda bb, hp, i: (bb, 0, hp)),
            pl.BlockSpec((None, None, None, 2, tq), lambda bb, hp, i: (bb, hp, i, 0, 0)),
            pl.BlockSpec((None, None, l, 2), lambda bb, hp, i: (bb, hp, 0, 0)),
        ],
        out_specs=pl.BlockSpec((None, tq, LANES), lambda bb, hp, i: (bb, i, hp)),
        out_shape=jax.ShapeDtypeStruct((b, l, fd), BF16),
        scratch_shapes=[pltpu.VMEM((LANES, 2 * tq), F32)],
        compiler_params=_cparams(3),
        name="fox_attention_prompt",
    )(q, k16, v16, f_rows, f_cols)


def _attn_sample_kernel(pt_ref, q_ref, kn_ref, vn_ref, lfn_ref, *refs, pps, page):
    ck = refs[0:pps]
    cv = refs[pps:2 * pps]
    clf = refs[2 * pps:3 * pps]
    o_ref = refs[3 * pps]
    m_ref, l_ref, acc_ref, carry_ref = refs[3 * pps + 1:]
    step = pl.program_id(1)
    nh, hd = FOX_HEADS, FOX_HD
    fd = nh * hd
    diag = (lax.broadcasted_iota(jnp.int32, (nh, fd), 1) // hd) == lax.broadcasted_iota(jnp.int32, (nh, fd), 0)
    q_bd = jnp.where(diag, jnp.broadcast_to(q_ref[...], (nh, fd)), 0.0)

    @pl.when(step == 0)
    def _():
        m_ref[...] = jnp.sum(q_bd * kn_ref[...], axis=-1, keepdims=True)
        l_ref[...] = jnp.ones_like(l_ref)
        acc_ref[...] = jnp.broadcast_to(vn_ref[...], (nh, fd))
        carry_ref[...] = lfn_ref[...]

    tt = lax.broadcasted_iota(jnp.int32, (page, 2 * page), 0)
    cc = lax.broadcasted_iota(jnp.int32, (page, 2 * page), 1)
    later = ((tt > cc) | (cc >= page)).astype(BF16)

    def split2(x):
        hi = x.astype(BF16)
        return jnp.concatenate([hi, (x - hi.astype(F32)).astype(BF16)], axis=0)

    q2 = split2(q_bd)
    lf_parts = [part for i in range(pps) for part in _split3(clf[i][...])]
    lfx_all = _dot(jnp.concatenate(lf_parts, axis=0), later)
    carry = carry_ref[...]
    scores = []
    for i in range(pps):
        base = 3 * nh * i
        lfx = (lfx_all[base:base + nh] + lfx_all[base + nh:base + 2 * nh]) + lfx_all[base + 2 * nh:base + 3 * nh]
        bias = lfx[:, :page] + carry
        carry = carry + lfx[:, page:page + 1]
        s2 = _dot(q2, ck[i][...].reshape(fd, page).astype(BF16))
        scores.append((s2[:nh] + s2[nh:]) + bias)
    carry_ref[...] = carry
    s = jnp.concatenate(scores, axis=1)
    m_old = m_ref[...]
    m_new = jnp.maximum(m_old, jnp.max(s, axis=-1, keepdims=True))
    p = jnp.exp(s - m_new)
    alpha = jnp.exp(m_old - m_new)
    m_ref[...] = m_new
    l_ref[...] = alpha * l_ref[...] + jnp.sum(p, axis=-1, keepdims=True)
    p2 = split2(p)
    pv = _dot_nt(p2[:, 0:page], cv[0][...].reshape(fd, page).astype(BF16))
    for i in range(1, pps):
        pv = pv + _dot_nt(p2[:, i * page:(i + 1) * page], cv[i][...].reshape(fd, page).astype(BF16))
    acc_ref[...] = alpha * acc_ref[...] + (pv[:nh] + pv[nh:])

    @pl.when(step == pl.num_programs(1) - 1)
    def _():
        o = acc_ref[...] / l_ref[...]
        o_ref[...] = jnp.sum(jnp.where(diag, o, 0.0), axis=0, keepdims=True)


def _attn_sample_call(page_table, q, k_new, v_new, lf_new, cache_kt, cache_vt, cache_lft, pps):
    nb, n_pages = page_table.shape
    n_pool, nh, hd, page = cache_kt.shape
    fd = nh * hd
    steps = n_pages // pps

    def page_map(i, nd):
        return lambda b, s, pt: (pt[b, n_pages - 1 - (s * pps + i)],) + (0,) * nd

    row = lambda r, w: pl.BlockSpec((None, r, w), lambda b, s, pt: (b, 0, 0))
    in_specs = [row(1, fd), row(1, fd), row(1, fd), row(nh, 1)]
    in_specs += [pl.BlockSpec((None, nh, hd, page), page_map(i, 3)) for i in range(pps)]
    in_specs += [pl.BlockSpec((None, nh, hd, page), page_map(i, 3)) for i in range(pps)]
    in_specs += [pl.BlockSpec((None, nh, page), page_map(i, 2)) for i in range(pps)]
    grid_spec = pltpu.PrefetchScalarGridSpec(
        num_scalar_prefetch=1,
        grid=(nb, steps),
        in_specs=in_specs,
        out_specs=row(1, fd),
        scratch_shapes=[pltpu.VMEM((nh, 1), F32), pltpu.VMEM((nh, 1), F32), pltpu.VMEM((nh, fd), F32),
                        pltpu.VMEM((nh, 1), F32)],
    )
    return pl.pallas_call(
        functools.partial(_attn_sample_kernel, pps=pps, page=page),
        grid_spec=grid_spec,
        out_shape=jax.ShapeDtypeStruct((nb, 1, fd), F32),
        compiler_params=_cparams(2),
        name="fox_attention_sample",
    )(page_table, q, k_new, v_new, lf_new, *([cache_kt] * pps), *([cache_vt] * pps), *([cache_lft] * pps))


def _rope_tables(pos):
    half = RET_DK // 2
    inv = ROPE_BASE ** (-jnp.arange(half, dtype=F32) / half)
    ang = pos.astype(F32)[:, None] * inv[None, :]
    return jnp.cos(ang), jnp.sin(ang)


def _pick_tile(n, pref):
    t = min(n, pref)
    while n % t:
        t //= 2
    return t


def kernel(x_prompt, x_sample, c_prompt, c_sample, state_ret, cache_k, cache_v, cache_logf, page_table, w_mod, b_mod, g_mix, g_ffn, w_ret_in, w_ret_out, g_kv, w_kvmod, b_kvmod, w_kv, w_f, b_f, w_fq, w_fo, w_rg, w_re, w_e_gate, w_e_up, w_e_down, g_final):
    b, l, d = x_prompt.shape
    nb = x_sample.shape[0]
    n_pool, page = cache_k.shape[0], cache_k.shape[1]
    n_pages = page_table.shape[1]
    past_len = n_pages * page
    fd = FOX_HEADS * FOX_HD

    c_all = jnp.concatenate([c_sample, c_prompt], axis=0)
    mod_all = _mod_call(c_all, w_mod, b_mod, 1536)
    kvmod_all = _mod_call(c_all, w_kvmod[None], b_kvmod[None], 1024)

    w_f_pad = jnp.pad(w_f, ((0, 0), (0, LANES - FOX_HEADS)))
    weights32 = dict(w_in=w_ret_in, w_out=w_ret_out[0], w_kv=w_kv, w_fq=w_fq[0], w_fo=w_fo[0], w_f=w_f_pad,
                     wg=w_e_gate, wu=w_e_up, wd=w_e_down)
    weights16 = {name: w.astype(BF16) for name, w in weights32.items()}
    b_f_row = jnp.pad(b_f, (0, LANES - FOX_HEADS)).reshape(1, LANES)
    w_r = jnp.pad(jnp.concatenate([w_re, w_rg], axis=-1), ((0, 0), (0, 0), (0, LANES - N_EXPERTS - N_GROUPS)))
    w_r_hi = w_r.astype(BF16)
    routers32 = jnp.stack([w_r, jnp.zeros_like(w_r)], axis=1)
    routers16 = jnp.stack([w_r_hi, (w_r - w_r_hi.astype(F32)).astype(BF16)], axis=1)
    g_mix2 = g_mix.reshape(-1, 1, d)
    g_ffn2 = g_ffn.reshape(-1, 1, d)
    g_kv2 = g_kv.reshape(1, d)
    g_fin2 = g_final.reshape(1, d)

    def trunk(x, mod, kvmod, pos, tm, tm_in, tm_moe, mixer0, mixer1, precise):
        w = weights32 if precise else weights16
        cos, sin = _rope_tables(pos)
        proj = _retin_call(x, mod, g_mix2[0], w["w_in"], cos, sin, tm_in, precise)
        o, s_new = mixer0(proj)
        def moe(h, gates, layer, x_res, final):
            if precise:
                return _moe_call(h, gates, w["wg"], w["wu"], w["wd"], layer, x_res, mod, g_fin2, tm_moe, final, True)
            return _moe_sorted_call(h, gates, w["wg"], w["wu"], w["wd"], layer, x_res, mod, g_fin2, tm_moe, final)

        w_router = routers32 if precise else routers16
        x1, h2, gates = _mixout_call(o, w["w_out"], x, mod, 0, g_ffn2[0], w_router[0], tm, precise)
        x2 = moe(h2, gates, 0, x1, False)
        k32, v32, k16, v16, lf, qf = _kvq_call(x2, kvmod, mod, g_kv2, w["w_kv"], w["w_f"], b_f_row, g_mix2[1],
                                               w["w_fq"], tm, precise)
        o1 = mixer1(qf, k32, v32, k16, v16, lf)
        x3, h4, gates1 = _mixout_call(o1, w["w_fo"], x2, mod, 1, g_ffn2[1], w_router[1], tm, precise)
        y = moe(h4, gates1, 1, x3, True)
        return y, s_new, k32, v32, lf

    tm_p = _pick_tile(l, 512)
    chunk = _pick_tile(l, 256)
    tq = _pick_tile(l, 512)

    def ret_prompt(proj):
        return _ret_prompt_call(proj, chunk)

    def fox_prompt(qf, k32, v32, k16, v16, lf):
        f_t = _cumsum_call(jnp.swapaxes(lf, 1, 2))
        f_pairs = f_t.reshape(b, FOX_HEADS // 2, 2, l)
        f_cols = jnp.swapaxes(f_pairs, 2, 3)
        f_rows = jnp.swapaxes(f_pairs.reshape(b, FOX_HEADS // 2, 2, l // tq, tq), 2, 3)
        return _fox_prompt_call(qf, k16, v16, f_rows, f_cols, tq)

    mod_p = _Mod(mod_all, False, nb, tm_p)
    kvmod_p = _Mod(kvmod_all, False, nb, tm_p)
    y_p, s_p, k_p, v_p, lf_p = trunk(x_prompt, mod_p, kvmod_p, jnp.arange(l), tm_p, _pick_tile(l, 1024),
                                     _pick_tile(l, 1024), ret_prompt, fox_prompt, False)

    def ret_sample(proj):
        o, s_new = _ret_sample_call(proj[0], state_ret[0])
        return o.reshape(1, nb, -1), s_new

    def fox_sample(qf, k32, v32, k16, v16, lf):
        rows = lambda a: a.reshape(nb, 1, fd)
        o = _attn_sample_call(page_table, rows(qf), rows(k32), rows(v32), lf.reshape(nb, FOX_HEADS, 1),
                              jnp.transpose(cache_k, (0, 2, 3, 1)), jnp.transpose(cache_v, (0, 2, 3, 1)),
                              jnp.swapaxes(cache_logf, 1, 2), _pick_tile(n_pages, 8))
        return o.reshape(1, nb, fd)

    mod_s = _Mod(mod_all, True, 0, nb)
    kvmod_s = _Mod(kvmod_all, True, 0, nb)
    y_s, s_s, k_s, v_s, lf_s = trunk(x_sample.reshape(1, nb, d), mod_s, kvmod_s,
                                     jnp.full((nb,), past_len, jnp.int32), nb, nb, nb, ret_sample, fox_sample, True)

    return (y_p, y_s.reshape(nb, 1, d), s_p[None], s_s[None],
            k_p.reshape(b, l, FOX_HEADS, FOX_HD), v_p.reshape(b, l, FOX_HEADS, FOX_HD), lf_p,
            k_s.reshape(nb, 1, FOX_HEADS, FOX_HD), v_s.reshape(nb, 1, FOX_HEADS, FOX_HD),
            lf_s.reshape(nb, 1, FOX_HEADS))
```

```python
import functools
import math

import jax
import jax.numpy as jnp
from jax import lax
from jax.experimental import pallas as pl
from jax.experimental.pallas import tpu as pltpu

D_MODEL = 1024
RET_HEADS = 4
RET_DK = 256
RET_DV = 512
ROPE_BASE = 10000.0
FOX_HEADS = 16
FOX_HD = 64
N_GROUPS = 4
EXPERTS_PER_GROUP = 8
N_EXPERTS = 32
D_EXPERT = 256
EPS = 1e-6
NEG_INF = -1e30
LOG2E = 1.4426950408889634

LANES = 128
VMEM_LIMIT = 56 * 1024 * 1024
GROUP_ID_LANE = 64
MOE_CHUNK = 128
MOE_WINDOW = 3
MOE_EXPERTS_PER_STEP = 4
F32 = jnp.float32
BF16 = jnp.bfloat16


def _cparams(n_axes):
    return pltpu.CompilerParams(dimension_semantics=("arbitrary",) * n_axes,
                                vmem_limit_bytes=VMEM_LIMIT)


def _silu(x):
    return x * jax.nn.sigmoid(x)


def _dot(a, b):
    return jnp.dot(a, b, preferred_element_type=F32)


def _dot_nt(a, b):
    return lax.dot_general(a, b, (((1,), (1,)), ((), ())), preferred_element_type=F32)


def _dot_tn(a, b):
    return lax.dot_general(a, b, (((0,), (0,)), ((), ())), preferred_element_type=F32)


def _mm(a, w, precise):
    if precise:
        a_hi = a.astype(BF16)
        a_lo = (a - a_hi.astype(F32)).astype(BF16)
        w_hi = w.astype(BF16)
        w_lo = (w - w_hi.astype(F32)).astype(BF16)
        return _dot(a_hi, w_hi) + (_dot(a_lo, w_hi) + _dot(a_hi, w_lo))
    return _dot(a.astype(BF16), w)


def _split3(x):
    hi = x.astype(BF16)
    r1 = x - hi.astype(F32)
    mid = r1.astype(BF16)
    lo = (r1 - mid.astype(F32)).astype(BF16)
    return hi, mid, lo


def _rms_mod(x, g, shift, scale):
    y = x * lax.rsqrt(jnp.mean(x * x, axis=-1, keepdims=True) + EPS)
    return (y * g) * (1.0 + scale) + shift


def _mod_kernel(c_ref, w_ref, b_ref, o_ref):
    o_ref[...] = _mm(_silu(c_ref[...]), w_ref[...], True) + b_ref[...]


def _mod_call(c, w, b, tn):
    ns, k, n = w.shape
    m = c.shape[0]
    return pl.pallas_call(
        _mod_kernel,
        grid=(ns, n // tn),
        in_specs=[
            pl.BlockSpec((m, k), lambda s, j: (0, 0)),
            pl.BlockSpec((None, k, tn), lambda s, j: (s, 0, j)),
            pl.BlockSpec((None, 1, tn), lambda s, j: (s, 0, j)),
        ],
        out_specs=pl.BlockSpec((None, m, tn), lambda s, j: (s, 0, j)),
        out_shape=jax.ShapeDtypeStruct((ns, m, n), F32),
        compiler_params=_cparams(2),
        name="adaln_mod",
    )(c, w, b.reshape(ns, 1, n))


class _Mod:
    def __init__(self, arr, per_token, row0, tm):
        self.per_token = per_token
        self.row0 = row0
        self.tm = tm
        self.arr = arr if per_token else arr.reshape(arr.shape[0], arr.shape[1], 1, arr.shape[2])

    def spec(self, stack, col):
        if self.per_token:
            assert self.row0 % self.tm == 0
            r0 = self.row0 // self.tm
            return pl.BlockSpec((None, self.tm, D_MODEL), lambda b, i: (stack, r0 + i, col))
        row0 = self.row0
        return pl.BlockSpec((None, None, 1, D_MODEL), lambda b, i: (stack, row0 + b, 0, col))


def _tok_spec(tm, width):
    return pl.BlockSpec((None, tm, width), lambda b, i: (b, i, 0))


def _const_spec(shape):
    nd = len(shape)
    return pl.BlockSpec(shape, lambda b, i: (0,) * nd)


def _retin_kernel(x_ref, sh_ref, sc_ref, g_ref, w_ref, cos_ref, sin_ref, o_ref, h_ref, *, precise):
    j = pl.program_id(2)

    @pl.when(j == 0)
    def _():
        h_ref[...] = _rms_mod(x_ref[...], g_ref[...], sh_ref[...], sc_ref[...]).astype(h_ref.dtype)

    p = _mm(h_ref[...], w_ref[...], precise)

    @pl.when(j < 2)
    def _():
        scale = jnp.where(j == 0, 1.0, RET_DK ** -0.5)
        half = RET_DK // 2
        cos = cos_ref[...]
        sin = sin_ref[...]
        for hh in range(RET_HEADS):
            a = hh * RET_DK
            x1 = p[:, a:a + half]
            x2 = p[:, a + half:a + RET_DK]
            o_ref[:, a:a + half] = ((x1 * cos - x2 * sin) * scale).astype(o_ref.dtype)
            o_ref[:, a + half:a + RET_DK] = ((x1 * sin + x2 * cos) * scale).astype(o_ref.dtype)

    @pl.when(j >= 2)
    def _():
        o_ref[...] = p.astype(o_ref.dtype)


def _retin_call(x, mod, g_mix, w_in, cos, sin, tm, precise):
    bx, lx, d = x.shape
    tn = RET_HEADS * RET_DK
    n = w_in.shape[-1]
    dt = F32 if precise else BF16
    spec3 = lambda sp: pl.BlockSpec(sp.block_shape, lambda b, i, j: sp.index_map(b, i))
    return pl.pallas_call(
        functools.partial(_retin_kernel, precise=precise),
        grid=(bx, lx // tm, n // tn),
        in_specs=[
            spec3(_tok_spec(tm, d)),
            spec3(mod.spec(0, 0)), spec3(mod.spec(0, 1)),
            spec3(_const_spec((1, d))),
            pl.BlockSpec((None, d, tn), lambda b, i, j: (0, 0, j)),
            pl.BlockSpec((tm, RET_DK // 2), lambda b, i, j: (i, 0)),
            pl.BlockSpec((tm, RET_DK // 2), lambda b, i, j: (i, 0)),
        ],
        out_specs=pl.BlockSpec((None, tm, tn), lambda b, i, j: (b, i, j)),
        out_shape=jax.ShapeDtypeStruct((bx, lx, n), dt),
        scratch_shapes=[pltpu.VMEM((tm, d), dt)],
        compiler_params=_cparams(3),
        name="ret_in_proj",
    )(x, mod.arr, mod.arr, g_mix, w_in, cos, sin)


def _log_gamma(h):
    return math.log(1.0 - 2.0 ** (-5.0 - h))


def _ret_prompt_kernel(q_ref, k_ref, v_ref, g_ref, o_ref, s_ref, *, chunk):
    @pl.when(pl.program_id(1) == 0)
    def _():
        s_ref[...] = jnp.zeros_like(s_ref)

    ti = lax.broadcasted_iota(jnp.int32, (chunk, chunk), 0)
    tj = lax.broadcasted_iota(jnp.int32, (chunk, chunk), 1)
    rel = (ti - tj).astype(F32)
    t = lax.broadcasted_iota(jnp.int32, (chunk, 1), 0).astype(F32)
    for h in range(RET_HEADS):
        lg = _log_gamma(h)
        decay = jnp.where(rel >= 0, jnp.exp(lg * jnp.maximum(rel, 0.0)), 0.0)
        qh = q_ref[:, h * RET_DK:(h + 1) * RET_DK]
        kh = k_ref[:, h * RET_DK:(h + 1) * RET_DK]
        vh = v_ref[:, h * RET_DV:(h + 1) * RET_DV]
        a = _dot_nt(qh, kh) * decay
        inner = _dot(a.astype(BF16), vh)
        s_old = s_ref[h]
        cross = _dot(qh, s_old.astype(BF16)) * jnp.exp((t + 1.0) * lg)
        o = inner + cross
        kd = (kh.astype(F32) * jnp.exp((chunk - 1.0 - t) * lg)).astype(BF16)
        s_ref[h] = math.exp(chunk * lg) * s_old + _dot_tn(kd, vh)
        o = o * lax.rsqrt(jnp.mean(o * o, axis=-1, keepdims=True) + EPS)
        gh = g_ref[:, h * RET_DV:(h + 1) * RET_DV].astype(F32)
        o_ref[:, h * RET_DV:(h + 1) * RET_DV] = (o * _silu(gh)).astype(BF16)


def _ret_prompt_call(proj, chunk):
    b, l, _ = proj.shape
    qd = RET_HEADS * RET_DK
    vd = RET_HEADS * RET_DV
    col = lambda w, j: pl.BlockSpec((None, chunk, w), lambda bb, c: (bb, c, j))
    return pl.pallas_call(
        functools.partial(_ret_prompt_kernel, chunk=chunk),
        grid=(b, l // chunk),
        in_specs=[col(qd, 0), col(qd, 1), col(vd, 1), col(vd, 2)],
        out_specs=[_tok_spec(chunk, vd),
                   pl.BlockSpec((None, RET_HEADS, RET_DK, RET_DV), lambda bb, c: (bb, 0, 0, 0))],
        out_shape=[jax.ShapeDtypeStruct((b, l, vd), BF16),
                   jax.ShapeDtypeStruct((b, RET_HEADS, RET_DK, RET_DV), F32)],
        compiler_params=_cparams(2),
        name="retention_prompt",
    )(proj, proj, proj, proj)


def _ret_sample_kernel(q_ref, k_ref, v_ref, g_ref, s_ref, o_ref, sn_ref):
    b = pl.program_id(0)
    h = pl.program_id(1)
    nb = q_ref.shape[0]
    gamma = 1.0 - 1.0 / (jnp.zeros((1, 1), F32) + (jnp.int32(32) << h).astype(F32))
    rowsel = lax.broadcasted_iota(jnp.int32, (nb, 1), 0) == b

    def row(ref):
        return jnp.sum(jnp.where(rowsel, ref[...], 0.0), axis=0, keepdims=True)

    eye = (lax.broadcasted_iota(jnp.int32, (RET_DK, RET_DK), 0)
           == lax.broadcasted_iota(jnp.int32, (RET_DK, RET_DK), 1))

    def column(r):
        return jnp.sum(jnp.where(eye, r, 0.0), axis=-1, keepdims=True)

    q = row(q_ref)
    k = row(k_ref)
    v = row(v_ref)
    s_old = s_ref[...]
    sn_ref[...] = gamma * s_old + column(k) * v
    qk = jnp.sum(q * k, axis=-1, keepdims=True)
    cross = jnp.sum(column(q) * s_old, axis=0, keepdims=True) * gamma
    o = qk * v + cross
    o = o * lax.rsqrt(jnp.mean(o * o, axis=-1, keepdims=True) + EPS)
    o_ref[...] = o * _silu(row(g_ref))


def _ret_sample_call(proj, s0):
    nb = proj.shape[0]
    k0 = RET_HEADS
    v0 = 2 * RET_HEADS * RET_DK // RET_DV
    g0 = v0 + RET_HEADS
    return pl.pallas_call(
        _ret_sample_kernel,
        grid=(nb, RET_HEADS),
        in_specs=[
            pl.BlockSpec((nb, RET_DK), lambda b, h: (0, h)),
            pl.BlockSpec((nb, RET_DK), lambda b, h: (0, k0 + h)),
            pl.BlockSpec((nb, RET_DV), lambda b, h: (0, v0 + h)),
            pl.BlockSpec((nb, RET_DV), lambda b, h: (0, g0 + h)),
            pl.BlockSpec((None, None, RET_DK, RET_DV), lambda b, h: (b, h, 0, 0)),
        ],
        out_specs=[pl.BlockSpec((None, 1, RET_DV), lambda b, h: (b, 0, h)),
                   pl.BlockSpec((None, None, RET_DK, RET_DV), lambda b, h: (b, h, 0, 0))],
        out_shape=[jax.ShapeDtypeStruct((nb, 1, RET_HEADS * RET_DV), F32),
                   jax.ShapeDtypeStruct((nb, RET_HEADS, RET_DK, RET_DV), F32)],
        compiler_params=_cparams(2),
        name="retention_sample",
    )(proj, proj, proj, proj, s0)


def _route(logits):
    tm = logits.shape[0]
    lane = lax.broadcasted_iota(jnp.int32, (tm, LANES), 1).astype(F32)
    ninf = -jnp.inf
    lg = jnp.where((lane >= N_EXPERTS) & (lane < N_EXPERTS + N_GROUPS), logits, ninf)
    mx = jnp.max(lg, axis=-1, keepdims=True)
    p_group = 1.0 / jnp.sum(jnp.exp(lg - mx), axis=-1, keepdims=True)
    gi = jnp.min(jnp.where(lg == mx, lane, float(LANES)), axis=-1, keepdims=True) - N_EXPERTS
    lo = gi * EXPERTS_PER_GROUP
    les = jnp.where((lane >= lo) & (lane < lo + EXPERTS_PER_GROUP), logits, ninf)
    v1 = jnp.max(les, axis=-1, keepdims=True)
    i1 = jnp.min(jnp.where(les == v1, lane, float(LANES)), axis=-1, keepdims=True)
    les2 = jnp.where(lane == i1, ninf, les)
    v2 = jnp.max(les2, axis=-1, keepdims=True)
    i2 = jnp.min(jnp.where(les2 == v2, lane, float(LANES)), axis=-1, keepdims=True)
    e2 = jnp.exp(v2 - v1)
    w1 = 1.0 / (1.0 + e2)
    w2 = e2 / (1.0 + e2)
    gates = jnp.where(lane == i1, w1 * p_group, 0.0) + jnp.where(lane == i2, w2 * p_group, 0.0)
    return gates + jnp.where(lane == GROUP_ID_LANE, gi, 0.0)


def _mixout_kernel(o_ref, w_ref, x_ref, g1_ref, sh_ref, sc_ref, gf_ref, wr_ref, x1_ref, h2_ref, gates_ref, *,
                   precise):
    y = _mm(o_ref[...], w_ref[...], precise)
    x1 = x_ref[...] + g1_ref[...] * y
    x1_ref[...] = x1
    h2 = _rms_mod(x1, gf_ref[...], sh_ref[...], sc_ref[...])
    h2_ref[...] = h2.astype(h2_ref.dtype)
    if precise:
        logits = _mm(h2, wr_ref[0], True)
    else:
        h_hi = h2.astype(BF16)
        h_lo = (h2 - h_hi.astype(F32)).astype(BF16)
        logits = _dot(h_hi, wr_ref[0]) + (_dot(h_lo, wr_ref[0]) + _dot(h_hi, wr_ref[1]))
    gates_ref[...] = _route(logits)


def _mixout_call(o, w_out, x, mod, layer, g_ffn, w_router, tm, precise):
    bx, lx, d = x.shape
    kd = o.shape[-1]
    return pl.pallas_call(
        functools.partial(_mixout_kernel, precise=precise),
        grid=(bx, lx // tm),
        in_specs=[
            _tok_spec(tm, kd),
            _const_spec((kd, d)),
            _tok_spec(tm, d),
            mod.spec(layer, 2), mod.spec(layer, 3), mod.spec(layer, 4),
            _const_spec((1, d)),
            _const_spec((2, d, LANES)),
        ],
        out_specs=[_tok_spec(tm, d), _tok_spec(tm, d), _tok_spec(tm, LANES)],
        out_shape=[jax.ShapeDtypeStruct((bx, lx, d), F32),
                   jax.ShapeDtypeStruct((bx, lx, d), F32 if precise else BF16),
                   jax.ShapeDtypeStruct((bx, lx, LANES), F32)],
        compiler_params=_cparams(2),
        name="mix_out_router",
    )(o, w_out, x, mod.arr, mod.arr, mod.arr, g_ffn, w_router)


def _moe_kernel(h_ref, gates_ref, wg_ref, wu_ref, wd_ref, x1_ref, g2_ref, gfin_ref, out_ref, acc_ref, *, final,
                precise):
    e = pl.program_id(2)

    @pl.when(e == 0)
    def _():
        acc_ref[...] = jnp.zeros_like(acc_ref)

    h = h_ref[...]
    a = _mm(h, wg_ref[...], precise)
    u = _mm(h, wu_ref[...], precise)
    lane = lax.broadcasted_iota(jnp.int32, gates_ref.shape, 1)
    ge = jnp.sum(jnp.where(lane == e, gates_ref[...], 0.0), axis=-1, keepdims=True)
    act = (_silu(a) * u) * ge
    acc_ref[...] += _mm(act, wd_ref[...], precise)

    @pl.when(e == pl.num_programs(2) - 1)
    def _():
        x2 = x1_ref[...] + g2_ref[...] * acc_ref[...]
        if final:
            x2 = (x2 * lax.rsqrt(jnp.mean(x2 * x2, axis=-1, keepdims=True) + EPS)) * gfin_ref[...]
        out_ref[...] = x2


def _moe_call(h2, gates, wg, wu, wd, layer, x1, mod, g_final, tm, final, precise):
    bx, lx, d = x1.shape
    tok3 = lambda w: pl.BlockSpec((None, tm, w), lambda b, i, e: (b, i, 0))
    g2_spec2 = mod.spec(layer, 5)
    g2_spec = pl.BlockSpec(g2_spec2.block_shape, lambda b, i, e: g2_spec2.index_map(b, i))
    return pl.pallas_call(
        functools.partial(_moe_kernel, final=final, precise=precise),
        grid=(bx, lx // tm, N_EXPERTS),
        in_specs=[
            tok3(d), tok3(LANES),
            pl.BlockSpec((None, d, D_EXPERT), lambda b, i, e: (layer, 0, e)),
            pl.BlockSpec((None, d, D_EXPERT), lambda b, i, e: (layer, 0, e)),
            pl.BlockSpec((None, D_EXPERT, d), lambda b, i, e: (layer, e, 0)),
            tok3(d),
            g2_spec,
            pl.BlockSpec((1, d), lambda b, i, e: (0, 0)),
        ],
        out_specs=tok3(d),
        out_shape=jax.ShapeDtypeStruct((bx, lx, d), F32),
        scratch_shapes=[pltpu.VMEM((tm, d), F32)],
        compiler_params=_cparams(3),
        name="hier_moe",
    )(h2, gates, wg, wu, wd, x1, mod.arr, g_final)


def _moe_plan_kernel(gates_ref, pos_ref, gs_ref, flags_ref, *, chunk):
    g = gates_ref[...]
    tm = g.shape[0]
    lane = lax.broadcasted_iota(jnp.int32, (tm, LANES), 1)
    gid = jnp.sum(jnp.where(lane == GROUP_ID_LANE, g, 0.0), axis=-1, keepdims=True)
    onehot = jnp.where((lane.astype(F32) == gid) & (lane < N_GROUPS), 1.0, 0.0)
    r = lax.broadcasted_iota(jnp.int32, (tm, tm), 0)
    c = lax.broadcasted_iota(jnp.int32, (tm, tm), 1)
    before = _dot((c < r).astype(BF16), onehot.astype(BF16))
    rank = jnp.sum(onehot * before, axis=-1, keepdims=True)
    count = jnp.broadcast_to(jnp.sum(onehot, axis=0, keepdims=True), (8, LANES))
    lower = (lax.broadcasted_iota(jnp.int32, (LANES, LANES), 0)
             < lax.broadcasted_iota(jnp.int32, (LANES, LANES), 1)).astype(BF16)
    hi, mid, lo = _split3(count)
    start = (_dot(hi, lower) + _dot(mid, lower)) + _dot(lo, lower)
    pos = jnp.sum(onehot * start[0:1], axis=-1, keepdims=True) + rank
    pos_ref[...] = jnp.broadcast_to(pos, (tm, LANES))
    place = (pos == c.astype(F32)).astype(BF16)
    hi, mid, lo = _split3(jnp.where(lane < N_EXPERTS, g, 0.0))
    gs = (_dot_tn(place, hi) + _dot_tn(place, mid)) + _dot_tn(place, lo)
    gs_ref[...] = gs
    flags_ref[...] = jnp.zeros_like(flags_ref)
    for k in range(tm // chunk):
        used = jnp.max(jnp.where(gs[k * chunk:(k + 1) * chunk] != 0.0, 1.0, 0.0), axis=0, keepdims=True)
        flags_ref[k:k + 1, :] = used.astype(jnp.int32)


def _moe_plan_call(gates, tm, chunk):
    bx, lx, _ = gates.shape
    n_chunks = tm // chunk
    rows = -(-n_chunks // 8) * 8
    tok = lambda: pl.BlockSpec((None, tm, LANES), lambda b, i: (b, i, 0))
    return pl.pallas_call(
        functools.partial(_moe_plan_kernel, chunk=chunk),
        grid=(bx, lx // tm),
        in_specs=[tok()],
        out_specs=[tok(), tok(), pl.BlockSpec((None, None, rows, LANES), lambda b, i: (b, i, 0, 0))],
        out_shape=[jax.ShapeDtypeStruct((bx, lx, LANES), F32), jax.ShapeDtypeStruct((bx, lx, LANES), F32),
                   jax.ShapeDtypeStruct((bx, lx // tm, rows, LANES), jnp.int32)],
        compiler_params=_cparams(2),
        name="moe_plan",
    )(gates)


def _moe_sorted_kernel(first_ref, nwin_ref, h_ref, pos_ref, gs_ref, wg_ref, wu_ref, wd_ref, x1_ref, g2_ref, gfin_ref,
                       out_ref, place_ref, xs_ref, gsp_ref, acc_ref, *, final, chunk, window, eps):
    b = pl.program_id(0)
    i = pl.program_id(1)
    step = pl.program_id(2)
    tm = h_ref.shape[0]
    tile = b * pl.num_programs(1) + i

    @pl.when(step == 0)
    def _():
        dest = lax.broadcasted_iota(jnp.int32, (tm, tm), 1).astype(F32)
        place = (pos_ref[:, 0:1] == dest).astype(BF16)
        place_ref[...] = place
        xs_ref[...] = jnp.zeros_like(xs_ref)
        xs_ref[0:tm, :] = _dot_tn(place, h_ref[...]).astype(BF16)
        gsp_ref[...] = jnp.zeros_like(gsp_ref)
        gsp_ref[0:tm, :] = gs_ref[...]
        acc_ref[...] = jnp.zeros_like(acc_ref)

    wrows = window * chunk
    lane = lax.broadcasted_iota(jnp.int32, (wrows, LANES), 1)

    for j in range(eps):
        e = step * eps + j
        cols = slice(j * D_EXPERT, (j + 1) * D_EXPERT)

        def one_window(w, carry, e=e, cols=cols):
            rows = pl.ds(pl.multiple_of((first_ref[tile, e] + w * window) * chunk, chunk), wrows)
            x = xs_ref[rows, :]
            ge = jnp.sum(jnp.where(lane == e, gsp_ref[rows, :], 0.0), axis=-1, keepdims=True)
            act = (_silu(_dot(x, wg_ref[:, cols])) * _dot(x, wu_ref[:, cols])) * ge
            acc_ref[rows, :] += _dot(act.astype(BF16), wd_ref[cols, :])
            return carry

        lax.fori_loop(0, nwin_ref[tile, e], one_window, 0)

    @pl.when(step == pl.num_programs(2) - 1)
    def _():
        x2 = x1_ref[...] + g2_ref[...] * _dot(place_ref[...], acc_ref[0:tm, :].astype(BF16))
        if final:
            x2 = (x2 * lax.rsqrt(jnp.mean(x2 * x2, axis=-1, keepdims=True) + EPS)) * gfin_ref[...]
        out_ref[...] = x2


def _moe_sorted_call(h2, gates, wg, wu, wd, layer, x1, mod, g_final, tm, final):
    bx, lx, d = x1.shape
    chunk = min(tm, MOE_CHUNK)
    n_chunks = tm // chunk
    window = min(n_chunks, MOE_WINDOW)
    pos, gs, flags = _moe_plan_call(gates, tm, chunk)
    used = flags[:, :, :n_chunks, :N_EXPERTS].reshape(-1, n_chunks, N_EXPERTS) != 0
    idx = jnp.arange(n_chunks, dtype=jnp.int32)[None, :, None]
    first = jnp.min(jnp.where(used, idx, n_chunks), axis=1)
    last = jnp.max(jnp.where(used, idx, -1), axis=1)
    nwin = jnp.where(last >= 0, (last - first) // window + 1, 0).astype(jnp.int32)
    first = jnp.where(last >= 0, first, 0).astype(jnp.int32)
    pad_rows = (window - 1) * chunk
    tok3 = lambda w: pl.BlockSpec((None, tm, w), lambda b, i, e, f, n: (b, i, 0))
    g2_spec2 = mod.spec(layer, 5)
    g2_spec = pl.BlockSpec(g2_spec2.block_shape, lambda b, i, e, f, n: g2_spec2.index_map(b, i))
    eps = MOE_EXPERTS_PER_STEP
    grid_spec = pltpu.PrefetchScalarGridSpec(
        num_scalar_prefetch=2,
        grid=(bx, lx // tm, N_EXPERTS // eps),
        in_specs=[
            tok3(d), tok3(LANES), tok3(LANES),
            pl.BlockSpec((None, d, eps * D_EXPERT), lambda b, i, e, f, n: (layer, 0, e)),
            pl.BlockSpec((None, d, eps * D_EXPERT), lambda b, i, e, f, n: (layer, 0, e)),
            pl.BlockSpec((None, eps * D_EXPERT, d), lambda b, i, e, f, n: (layer, e, 0)),
            tok3(d),
            g2_spec,
            pl.BlockSpec((1, d), lambda b, i, e, f, n: (0, 0)),
        ],
        out_specs=tok3(d),
        scratch_shapes=[pltpu.VMEM((tm, tm), BF16), pltpu.VMEM((tm + pad_rows, d), BF16),
                        pltpu.VMEM((tm + pad_rows, LANES), F32), pltpu.VMEM((tm + pad_rows, d), F32)],
    )
    return pl.pallas_call(
        functools.partial(_moe_sorted_kernel, final=final, chunk=chunk, window=window, eps=eps),
        grid_spec=grid_spec,
        out_shape=jax.ShapeDtypeStruct((bx, lx, d), F32),
        compiler_params=_cparams(3),
        name="hier_moe_sorted",
    )(first, nwin, h2, pos, gs, wg, wu, wd, x1, mod.arr, g_final)


def _log_sigmoid(z):
    return -(jnp.maximum(-z, 0.0) + jnp.log1p(jnp.exp(-jnp.abs(z))))


def _kvq_kernel(x_ref, shk_ref, sck_ref, gkv_ref, wkv_ref, wf_ref, bf_ref, sh1_ref, sc1_ref, gmix_ref, wq_ref,
                k32_ref, v32_ref, k16_ref, v16_ref, lf_ref, q_ref, *, precise, q_scale):
    x = x_ref[...]
    fd = FOX_HEADS * FOX_HD
    n = _rms_mod(x, gkv_ref[...], shk_ref[...], sck_ref[...])
    if not precise:
        n = n.astype(BF16)
    k = _mm(n, wkv_ref[:, 0:fd], precise)
    k32_ref[...] = k
    k16_ref[...] = k.astype(BF16)
    v = _mm(n, wkv_ref[:, fd:2 * fd], precise)
    v32_ref[...] = v
    v16_ref[...] = v.astype(BF16)
    z = _mm(n, wf_ref[...], precise) + bf_ref[...]
    lf_ref[...] = _log_sigmoid(z)[:, :FOX_HEADS]
    h = _rms_mod(x, gmix_ref[...], sh1_ref[...], sc1_ref[...])
    q_ref[...] = (_mm(h, wq_ref[...], precise) * q_scale).astype(q_ref.dtype)


def _kvq_call(x, kvmod, mod, g_kv, w_kv, w_f, b_f, g_mix, w_q, tm, precise, q_scale):
    bx, lx, d = x.shape
    fd = FOX_HEADS * FOX_HD
    return pl.pallas_call(
        functools.partial(_kvq_kernel, precise=precise, q_scale=q_scale),
        grid=(bx, lx // tm),
        in_specs=[
            _tok_spec(tm, d),
            kvmod.spec(0, 0), kvmod.spec(0, 1),
            _const_spec((1, d)),
            _const_spec((d, 2 * fd)),
            _const_spec((d, LANES)),
            _const_spec((1, LANES)),
            mod.spec(1, 0), mod.spec(1, 1),
            _const_spec((1, d)),
            _const_spec((d, fd)),
        ],
        out_specs=[_tok_spec(tm, fd), _tok_spec(tm, fd), _tok_spec(tm, fd), _tok_spec(tm, fd),
                   _tok_spec(tm, FOX_HEADS), _tok_spec(tm, fd)],
        out_shape=[jax.ShapeDtypeStruct((bx, lx, fd), F32), jax.ShapeDtypeStruct((bx, lx, fd), F32),
                   jax.ShapeDtypeStruct((bx, lx, fd), BF16), jax.ShapeDtypeStruct((bx, lx, fd), BF16),
                   jax.ShapeDtypeStruct((bx, lx, FOX_HEADS), F32),
                   jax.ShapeDtypeStruct((bx, lx, fd), F32 if precise else BF16)],
        compiler_params=_cparams(2),
        name="kv_q_proj",
    )(x, kvmod.arr, kvmod.arr, g_kv, w_kv, w_f, b_f, mod.arr, mod.arr, g_mix, w_q)


def _cumsum_kernel(x_ref, o_ref):
    nh, l = x_ref.shape
    r = lax.broadcasted_iota(jnp.int32, (LANES, LANES), 0)
    c = lax.broadcasted_iota(jnp.int32, (LANES, LANES), 1)
    upper = (r <= c).astype(BF16)
    carry = jnp.zeros((nh, 1), F32)
    for blk in range(l // LANES):
        hi, mid, lo = _split3(x_ref[:, blk * LANES:(blk + 1) * LANES])
        cs = (_dot(hi, upper) + _dot(mid, upper)) + _dot(lo, upper) + carry
        o_ref[:, blk * LANES:(blk + 1) * LANES] = cs
        carry = cs[:, LANES - 1:LANES]


def _cumsum_call(x):
    b, nh, l = x.shape
    return pl.pallas_call(
        _cumsum_kernel,
        grid=(b,),
        in_specs=[pl.BlockSpec((None, nh, l), lambda i: (i, 0, 0))],
        out_specs=pl.BlockSpec((None, nh, l), lambda i: (i, 0, 0)),
        out_shape=jax.ShapeDtypeStruct((b, nh, l), F32),
        compiler_params=_cparams(1),
        name="logf_cumsum",
    )(x)


def _fox_kernel(q_ref, k_ref, v_ref, fq_ref, fk_ref, o_ref, acc_ref, *, t):
    qi = pl.program_id(2)
    first = lax.broadcasted_iota(jnp.int32, (1, LANES), 1) < FOX_HD
    col_a = lax.broadcasted_iota(jnp.int32, (1, 2 * t), 1) < t
    q2 = q_ref[...]
    zero = jnp.zeros_like(q2)
    qs = jnp.concatenate([jnp.where(first, q2, zero), jnp.where(first, zero, q2)], axis=0)
    fq = jnp.concatenate([fq_ref[0:1, :], fq_ref[1:2, :]], axis=1) * LOG2E
    key = lax.broadcasted_iota(jnp.int32, (t, 2 * t), 0)
    qry = lax.broadcasted_iota(jnp.int32, (t, 2 * t), 1)
    causal = key <= jnp.where(col_a, qry, qry - t)
    acc_ref[...] = jnp.zeros_like(acc_ref)

    def tile(j, m, l, diagonal):
        start = pl.multiple_of(j * t, t)
        kt = k_ref[pl.ds(start, t), :]
        vt = v_ref[pl.ds(start, t), :]
        fk = fk_ref[pl.ds(start, t), :] * LOG2E
        s = (_dot_nt(kt, qs) + fq) - jnp.where(col_a, fk[:, 0:1], fk[:, 1:2])
        if diagonal:
            s = jnp.where(causal, s, NEG_INF)
        m_new = jnp.maximum(m, jnp.max(s, axis=0, keepdims=True))
        p = jnp.exp2(s - m_new)
        alpha = jnp.exp2(m - m_new)
        acc_ref[...] = alpha * acc_ref[...] + _dot_tn(vt, p.astype(BF16))
        return m_new, alpha * l + jnp.sum(p, axis=0, keepdims=True)

    m0 = jnp.full((1, 2 * t), -jnp.inf, F32)
    l0 = jnp.zeros((1, 2 * t), F32)
    m, l = lax.fori_loop(0, qi, lambda j, c: tile(j, c[0], c[1], False), (m0, l0))
    m, l = tile(qi, m, l, True)
    o_t = acc_ref[...] / l
    row_a = lax.broadcasted_iota(jnp.int32, (LANES, 1), 0) < FOX_HD
    o_ref[...] = jnp.where(row_a, o_t[:, :t], o_t[:, t:]).T.astype(BF16)


def _fox_prompt_call(q, k16, v16, f_rows, f_cols, tq):
    b, l, fd = q.shape
    npair = fd // LANES
    return pl.pallas_call(
        functools.partial(_fox_kernel, t=tq),
        grid=(b, npair, l // tq),
        in_specs=[
            pl.BlockSpec((None, tq, LANES), lambda bb, hp, i: (bb, i, hp)),
            pl.BlockSpec((None, l, LANES), lambda bb, hp, i: (bb, 0, hp)),
            pl.BlockSpec((None, l, LANES), lambda bb, hp, i: (bb, 0, hp)),
            pl.BlockSpec((None, None, None, 2, tq), lambda bb, hp, i: (bb, hp, i, 0, 0)),
            pl.BlockSpec((None, None, l, 2), lambda bb, hp, i: (bb, hp, 0, 0)),
        ],
        out_specs=pl.BlockSpec((None, tq, LANES), lambda bb, hp, i: (bb, i, hp)),
        out_shape=jax.ShapeDtypeStruct((b, l, fd), BF16),
        scratch_shapes=[pltpu.VMEM((LANES, 2 * tq), F32)],
        compiler_params=_cparams(3),
        name="fox_attention_prompt",
    )(q, k16, v16, f_rows, f_cols)


def _attn_sample_kernel(pt_ref, q_ref, kn_ref, vn_ref, lfn_ref, *refs, pps, page):
    ck = refs[0:pps]
    cv = refs[pps:2 * pps]
    clf = refs[2 * pps:3 * pps]
    o_ref = refs[3 * pps]
    m_ref, l_ref, acc_ref, carry_ref = refs[3 * pps + 1:]
    step = pl.program_id(1)
    nh, hd = FOX_HEADS, FOX_HD
    fd = nh * hd
    diag = (lax.broadcasted_iota(jnp.int32, (nh, fd), 1) // hd) == lax.broadcasted_iota(jnp.int32, (nh, fd), 0)
    q_bd = jnp.where(diag, jnp.broadcast_to(q_ref[...], (nh, fd)), 0.0)

    @pl.when(step == 0)
    def _():
        m_ref[...] = jnp.sum(q_bd * kn_ref[...], axis=-1, keepdims=True)
        l_ref[...] = jnp.ones_like(l_ref)
        acc_ref[...] = jnp.broadcast_to(vn_ref[...], (nh, fd))
        carry_ref[...] = lfn_ref[...]

    tt = lax.broadcasted_iota(jnp.int32, (page, 2 * page), 0)
    cc = lax.broadcasted_iota(jnp.int32, (page, 2 * page), 1)
    later = ((tt > cc) | (cc >= page)).astype(BF16)

    def split2(x):
        hi = x.astype(BF16)
        return jnp.concatenate([hi, (x - hi.astype(F32)).astype(BF16)], axis=0)

    q2 = split2(q_bd)
    lf_parts = [part for i in range(pps) for part in _split3(clf[i][...])]
    lfx_all = _dot(jnp.concatenate(lf_parts, axis=0), later)
    carry = carry_ref[...]
    scores = []
    for i in range(pps):
        base = 3 * nh * i
        lfx = (lfx_all[base:base + nh] + lfx_all[base + nh:base + 2 * nh]) + lfx_all[base + 2 * nh:base + 3 * nh]
        bias = lfx[:, :page] + carry
        carry = carry + lfx[:, page:page + 1]
        s2 = _dot(q2, ck[i][...].reshape(fd, page).astype(BF16))
        scores.append((s2[:nh] + s2[nh:]) + bias)
    carry_ref[...] = carry
    s = jnp.concatenate(scores, axis=1)
    m_old = m_ref[...]
    m_new = jnp.maximum(m_old, jnp.max(s, axis=-1, keepdims=True))
    p = jnp.exp(s - m_new)
    alpha = jnp.exp(m_old - m_new)
    m_ref[...] = m_new
    l_ref[...] = alpha * l_ref[...] + jnp.sum(p, axis=-1, keepdims=True)
    p2 = split2(p)
    pv = _dot_nt(p2[:, 0:page], cv[0][...].reshape(fd, page).astype(BF16))
    for i in range(1, pps):
        pv = pv + _dot_nt(p2[:, i * page:(i + 1) * page], cv[i][...].reshape(fd, page).astype(BF16))
    acc_ref[...] = alpha * acc_ref[...] + (pv[:nh] + pv[nh:])

    @pl.when(step == pl.num_programs(1) - 1)
    def _():
        o = acc_ref[...] / l_ref[...]
        o_ref[...] = jnp.sum(jnp.where(diag, o, 0.0), axis=0, keepdims=True)


def _attn_sample_call(page_table, q, k_new, v_new, lf_new, cache_kt, cache_vt, cache_lft, pps):
    nb, n_pages = page_table.shape
    n_pool, nh, hd, page = cache_kt.shape
    fd = nh * hd
    steps = n_pages // pps

    def page_map(i, nd):
        return lambda b, s, pt: (pt[b, n_pages - 1 - (s * pps + i)],) + (0,) * nd

    row = lambda r, w: pl.BlockSpec((None, r, w), lambda b, s, pt: (b, 0, 0))
    in_specs = [row(1, fd), row(1, fd), row(1, fd), row(nh, 1)]
    in_specs += [pl.BlockSpec((None, nh, hd, page), page_map(i, 3)) for i in range(pps)]
    in_specs += [pl.BlockSpec((None, nh, hd, page), page_map(i, 3)) for i in range(pps)]
    in_specs += [pl.BlockSpec((None, nh, page), page_map(i, 2)) for i in range(pps)]
    grid_spec = pltpu.PrefetchScalarGridSpec(
        num_scalar_prefetch=1,
        grid=(nb, steps),
        in_specs=in_specs,
        out_specs=row(1, fd),
        scratch_shapes=[pltpu.VMEM((nh, 1), F32), pltpu.VMEM((nh, 1), F32), pltpu.VMEM((nh, fd), F32),
                        pltpu.VMEM((nh, 1), F32)],
    )
    return pl.pallas_call(
        functools.partial(_attn_sample_kernel, pps=pps, page=page),
        grid_spec=grid_spec,
        out_shape=jax.ShapeDtypeStruct((nb, 1, fd), F32),
        compiler_params=_cparams(2),
        name="fox_attention_sample",
    )(page_table, q, k_new, v_new, lf_new, *([cache_kt] * pps), *([cache_vt] * pps), *([cache_lft] * pps))


def _rope_tables(pos):
    half = RET_DK // 2
    inv = ROPE_BASE ** (-jnp.arange(half, dtype=F32) / half)
    ang = pos.astype(F32)[:, None] * inv[None, :]
    return jnp.cos(ang), jnp.sin(ang)


def _pick_tile(n, pref):
    t = min(n, pref)
    while n % t:
        t //= 2
    return t


def kernel(x_prompt, x_sample, c_prompt, c_sample, state_ret, cache_k, cache_v, cache_logf, page_table, w_mod, b_mod, g_mix, g_ffn, w_ret_in, w_ret_out, g_kv, w_kvmod, b_kvmod, w_kv, w_f, b_f, w_fq, w_fo, w_rg, w_re, w_e_gate, w_e_up, w_e_down, g_final):
    b, l, d = x_prompt.shape
    nb = x_sample.shape[0]
    n_pool, page = cache_k.shape[0], cache_k.shape[1]
    n_pages = page_table.shape[1]
    past_len = n_pages * page
    fd = FOX_HEADS * FOX_HD

    c_all = jnp.concatenate([c_sample, c_prompt], axis=0)
    mod_all = _mod_call(c_all, w_mod, b_mod, 1536)
    kvmod_all = _mod_call(c_all, w_kvmod[None], b_kvmod[None], 1024)

    w_f_pad = jnp.pad(w_f, ((0, 0), (0, LANES - FOX_HEADS)))
    weights32 = dict(w_in=w_ret_in, w_out=w_ret_out[0], w_kv=w_kv, w_fq=w_fq[0], w_fo=w_fo[0], w_f=w_f_pad,
                     wg=w_e_gate, wu=w_e_up, wd=w_e_down)
    weights16 = {name: w.astype(BF16) for name, w in weights32.items()}
    b_f_row = jnp.pad(b_f, (0, LANES - FOX_HEADS)).reshape(1, LANES)
    w_r = jnp.pad(jnp.concatenate([w_re, w_rg], axis=-1), ((0, 0), (0, 0), (0, LANES - N_EXPERTS - N_GROUPS)))
    w_r_hi = w_r.astype(BF16)
    routers32 = jnp.stack([w_r, jnp.zeros_like(w_r)], axis=1)
    routers16 = jnp.stack([w_r_hi, (w_r - w_r_hi.astype(F32)).astype(BF16)], axis=1)
    g_mix2 = g_mix.reshape(-1, 1, d)
    g_ffn2 = g_ffn.reshape(-1, 1, d)
    g_kv2 = g_kv.reshape(1, d)
    g_fin2 = g_final.reshape(1, d)

    def trunk(x, mod, kvmod, pos, tm, tm_in, tm_moe, mixer0, mixer1, precise, q_scale):
        w = weights32 if precise else weights16
        cos, sin = _rope_tables(pos)
        proj = _retin_call(x, mod, g_mix2[0], w["w_in"], cos, sin, tm_in, precise)
        o, s_new = mixer0(proj)
        def moe(h, gates, layer, x_res, final):
            if precise:
                return _moe_call(h, gates, w["wg"], w["wu"], w["wd"], layer, x_res, mod, g_fin2, tm_moe, final, True)
            return _moe_sorted_call(h, gates, w["wg"], w["wu"], w["wd"], layer, x_res, mod, g_fin2, tm_moe, final)

        w_router = routers32 if precise else routers16
        x1, h2, gates = _mixout_call(o, w["w_out"], x, mod, 0, g_ffn2[0], w_router[0], tm, precise)
        x2 = moe(h2, gates, 0, x1, False)
        k32, v32, k16, v16, lf, qf = _kvq_call(x2, kvmod, mod, g_kv2, w["w_kv"], w["w_f"], b_f_row, g_mix2[1],
                                               w["w_fq"], tm, precise, q_scale)
        o1 = mixer1(qf, k32, v32, k16, v16, lf)
        x3, h4, gates1 = _mixout_call(o1, w["w_fo"], x2, mod, 1, g_ffn2[1], w_router[1], tm, precise)
        y = moe(h4, gates1, 1, x3, True)
        return y, s_new, k32, v32, lf

    tm_p = _pick_tile(l, 512)
    chunk = _pick_tile(l, 256)
    tq = _pick_tile(l, 512)

    def ret_prompt(proj):
        return _ret_prompt_call(proj, chunk)

    def fox_prompt(qf, k32, v32, k16, v16, lf):
        f_t = _cumsum_call(jnp.swapaxes(lf, 1, 2))
        f_pairs = f_t.reshape(b, FOX_HEADS // 2, 2, l)
        f_cols = jnp.swapaxes(f_pairs, 2, 3)
        f_rows = jnp.swapaxes(f_pairs.reshape(b, FOX_HEADS // 2, 2, l // tq, tq), 2, 3)
        return _fox_prompt_call(qf, k16, v16, f_rows, f_cols, tq)

    mod_p = _Mod(mod_all, False, nb, tm_p)
    kvmod_p = _Mod(kvmod_all, False, nb, tm_p)
    y_p, s_p, k_p, v_p, lf_p = trunk(x_prompt, mod_p, kvmod_p, jnp.arange(l), tm_p, _pick_tile(l, 1024),
                                     _pick_tile(l, 1024), ret_prompt, fox_prompt, False, FOX_HD ** -0.5 * LOG2E)

    def ret_sample(proj):
        o, s_new = _ret_sample_call(proj[0], state_ret[0])
        return o.reshape(1, nb, -1), s_new

    def fox_sample(qf, k32, v32, k16, v16, lf):
        rows = lambda a: a.reshape(nb, 1, fd)
        o = _attn_sample_call(page_table, rows(qf), rows(k32), rows(v32), lf.reshape(nb, FOX_HEADS, 1),
                              jnp.transpose(cache_k, (0, 2, 3, 1)), jnp.transpose(cache_v, (0, 2, 3, 1)),
                              jnp.swapaxes(cache_logf, 1, 2), _pick_tile(n_pages, 8))
        return o.reshape(1, nb, fd)

    mod_s = _Mod(mod_all, True, 0, nb)
    kvmod_s = _Mod(kvmod_all, True, 0, nb)
    y_s, s_s, k_s, v_s, lf_s = trunk(x_sample.reshape(1, nb, d), mod_s, kvmod_s,
                                     jnp.full((nb,), past_len, jnp.int32), nb, nb, nb, ret_sample, fox_sample, True,
                                     FOX_HD ** -0.5)

    return (y_p, y_s.reshape(nb, 1, d), s_p[None], s_s[None],
            k_p.reshape(b, l, FOX_HEADS, FOX_HD), v_p.reshape(b, l, FOX_HEADS, FOX_HD), lf_p,
            k_s.reshape(nb, 1, FOX_HEADS, FOX_HD), v_s.reshape(nb, 1, FOX_HEADS, FOX_HD),
            lf_s.reshape(nb, 1, FOX_HEADS))
```

```python
import functools
import math

import jax
import jax.numpy as jnp
from jax import lax
from jax.experimental import pallas as pl
from jax.experimental.pallas import tpu as pltpu

D_MODEL = 1024
RET_HEADS = 4
RET_DK = 256
RET_DV = 512
ROPE_BASE = 10000.0
FOX_HEADS = 16
FOX_HD = 64
N_GROUPS = 4
EXPERTS_PER_GROUP = 8
N_EXPERTS = 32
D_EXPERT = 256
EPS = 1e-6
NEG_INF = -1e30
LOG2E = 1.4426950408889634

LANES = 128
VMEM_LIMIT = 56 * 1024 * 1024
GROUP_ID_LANE = 64
MOE_CHUNK = 128
MOE_WINDOW = 3
MOE_EXPERTS_PER_STEP = 4
F32 = jnp.float32
BF16 = jnp.bfloat16


def _cparams(n_axes):
    return pltpu.CompilerParams(dimension_semantics=("arbitrary",) * n_axes,
                                vmem_limit_bytes=VMEM_LIMIT)


def _silu(x):
    return x * jax.nn.sigmoid(x)


def _dot(a, b):
    return jnp.dot(a, b, preferred_element_type=F32)


def _dot_nt(a, b):
    return lax.dot_general(a, b, (((1,), (1,)), ((), ())), preferred_element_type=F32)


def _dot_tn(a, b):
    return lax.dot_general(a, b, (((0,), (0,)), ((), ())), preferred_element_type=F32)


def _mm(a, w, precise):
    if precise:
        a_hi = a.astype(BF16)
        a_lo = (a - a_hi.astype(F32)).astype(BF16)
        w_hi = w.astype(BF16)
        w_lo = (w - w_hi.astype(F32)).astype(BF16)
        return _dot(a_hi, w_hi) + (_dot(a_lo, w_hi) + _dot(a_hi, w_lo))
    return _dot(a.astype(BF16), w)


def _split3(x):
    hi = x.astype(BF16)
    r1 = x - hi.astype(F32)
    mid = r1.astype(BF16)
    lo = (r1 - mid.astype(F32)).astype(BF16)
    return hi, mid, lo


def _rms_mod(x, g, shift, scale):
    y = x * lax.rsqrt(jnp.mean(x * x, axis=-1, keepdims=True) + EPS)
    return (y * g) * (1.0 + scale) + shift


def _mod_kernel(c_ref, w_ref, b_ref, o_ref):
    o_ref[...] = _mm(_silu(c_ref[...]), w_ref[...], True) + b_ref[...]


def _mod_call(c, w, b, tn):
    ns, k, n = w.shape
    m = c.shape[0]
    return pl.pallas_call(
        _mod_kernel,
        grid=(ns, n // tn),
        in_specs=[
            pl.BlockSpec((m, k), lambda s, j: (0, 0)),
            pl.BlockSpec((None, k, tn), lambda s, j: (s, 0, j)),
            pl.BlockSpec((None, 1, tn), lambda s, j: (s, 0, j)),
        ],
        out_specs=pl.BlockSpec((None, m, tn), lambda s, j: (s, 0, j)),
        out_shape=jax.ShapeDtypeStruct((ns, m, n), F32),
        compiler_params=_cparams(2),
        name="adaln_mod",
    )(c, w, b.reshape(ns, 1, n))


class _Mod:
    def __init__(self, arr, per_token, row0, tm):
        self.per_token = per_token
        self.row0 = row0
        self.tm = tm
        self.arr = arr if per_token else arr.reshape(arr.shape[0], arr.shape[1], 1, arr.shape[2])

    def spec(self, stack, col):
        if self.per_token:
            assert self.row0 % self.tm == 0
            r0 = self.row0 // self.tm
            return pl.BlockSpec((None, self.tm, D_MODEL), lambda b, i: (stack, r0 + i, col))
        row0 = self.row0
        return pl.BlockSpec((None, None, 1, D_MODEL), lambda b, i: (stack, row0 + b, 0, col))


def _tok_spec(tm, width):
    return pl.BlockSpec((None, tm, width), lambda b, i: (b, i, 0))


def _const_spec(shape):
    nd = len(shape)
    return pl.BlockSpec(shape, lambda b, i: (0,) * nd)


def _retin_kernel(x_ref, sh_ref, sc_ref, g_ref, w_ref, cos_ref, sin_ref, o_ref, h_ref, *, precise):
    j = pl.program_id(2)

    @pl.when(j == 0)
    def _():
        h_ref[...] = _rms_mod(x_ref[...], g_ref[...], sh_ref[...], sc_ref[...]).astype(h_ref.dtype)

    p = _mm(h_ref[...], w_ref[...], precise)

    @pl.when(j < 2)
    def _():
        scale = jnp.where(j == 0, 1.0, RET_DK ** -0.5)
        half = RET_DK // 2
        cos = cos_ref[...]
        sin = sin_ref[...]
        for hh in range(RET_HEADS):
            a = hh * RET_DK
            x1 = p[:, a:a + half]
            x2 = p[:, a + half:a + RET_DK]
            o_ref[:, a:a + half] = ((x1 * cos - x2 * sin) * scale).astype(o_ref.dtype)
            o_ref[:, a + half:a + RET_DK] = ((x1 * sin + x2 * cos) * scale).astype(o_ref.dtype)

    @pl.when(j >= 2)
    def _():
        o_ref[...] = p.astype(o_ref.dtype)


def _retin_call(x, mod, g_mix, w_in, cos, sin, tm, precise):
    bx, lx, d = x.shape
    tn = RET_HEADS * RET_DK
    n = w_in.shape[-1]
    dt = F32 if precise else BF16
    spec3 = lambda sp: pl.BlockSpec(sp.block_shape, lambda b, i, j: sp.index_map(b, i))
    return pl.pallas_call(
        functools.partial(_retin_kernel, precise=precise),
        grid=(bx, lx // tm, n // tn),
        in_specs=[
            spec3(_tok_spec(tm, d)),
            spec3(mod.spec(0, 0)), spec3(mod.spec(0, 1)),
            spec3(_const_spec((1, d))),
            pl.BlockSpec((None, d, tn), lambda b, i, j: (0, 0, j)),
            pl.BlockSpec((tm, RET_DK // 2), lambda b, i, j: (i, 0)),
            pl.BlockSpec((tm, RET_DK // 2), lambda b, i, j: (i, 0)),
        ],
        out_specs=pl.BlockSpec((None, tm, tn), lambda b, i, j: (b, i, j)),
        out_shape=jax.ShapeDtypeStruct((bx, lx, n), dt),
        scratch_shapes=[pltpu.VMEM((tm, d), dt)],
        compiler_params=_cparams(3),
        name="ret_in_proj",
    )(x, mod.arr, mod.arr, g_mix, w_in, cos, sin)


def _log_gamma(h):
    return math.log(1.0 - 2.0 ** (-5.0 - h))


def _ret_prompt_kernel(q_ref, k_ref, v_ref, g_ref, o_ref, s_ref, *, chunk):
    @pl.when(pl.program_id(1) == 0)
    def _():
        s_ref[...] = jnp.zeros_like(s_ref)

    ti = lax.broadcasted_iota(jnp.int32, (chunk, chunk), 0)
    tj = lax.broadcasted_iota(jnp.int32, (chunk, chunk), 1)
    rel = (ti - tj).astype(F32)
    t = lax.broadcasted_iota(jnp.int32, (chunk, 1), 0).astype(F32)
    for h in range(RET_HEADS):
        lg = _log_gamma(h)
        decay = jnp.where(rel >= 0, jnp.exp(lg * jnp.maximum(rel, 0.0)), 0.0)
        qh = q_ref[:, h * RET_DK:(h + 1) * RET_DK]
        kh = k_ref[:, h * RET_DK:(h + 1) * RET_DK]
        vh = v_ref[:, h * RET_DV:(h + 1) * RET_DV]
        a = _dot_nt(qh, kh) * decay
        inner = _dot(a.astype(BF16), vh)
        s_old = s_ref[h]
        cross = _dot(qh, s_old.astype(BF16)) * jnp.exp((t + 1.0) * lg)
        o = inner + cross
        kd = (kh.astype(F32) * jnp.exp((chunk - 1.0 - t) * lg)).astype(BF16)
        s_ref[h] = math.exp(chunk * lg) * s_old + _dot_tn(kd, vh)
        o = o * lax.rsqrt(jnp.mean(o * o, axis=-1, keepdims=True) + EPS)
        gh = g_ref[:, h * RET_DV:(h + 1) * RET_DV].astype(F32)
        o_ref[:, h * RET_DV:(h + 1) * RET_DV] = (o * _silu(gh)).astype(BF16)


def _ret_prompt_call(proj, chunk):
    b, l, _ = proj.shape
    qd = RET_HEADS * RET_DK
    vd = RET_HEADS * RET_DV
    col = lambda w, j: pl.BlockSpec((None, chunk, w), lambda bb, c: (bb, c, j))
    return pl.pallas_call(
        functools.partial(_ret_prompt_kernel, chunk=chunk),
        grid=(b, l // chunk),
        in_specs=[col(qd, 0), col(qd, 1), col(vd, 1), col(vd, 2)],
        out_specs=[_tok_spec(chunk, vd),
                   pl.BlockSpec((None, RET_HEADS, RET_DK, RET_DV), lambda bb, c: (bb, 0, 0, 0))],
        out_shape=[jax.ShapeDtypeStruct((b, l, vd), BF16),
                   jax.ShapeDtypeStruct((b, RET_HEADS, RET_DK, RET_DV), F32)],
        compiler_params=_cparams(2),
        name="retention_prompt",
    )(proj, proj, proj, proj)


def _ret_sample_kernel(p_ref, s_ref, o_ref, sn_ref):
    b = pl.program_id(0)
    nb = p_ref.shape[0]
    qd = RET_HEADS * RET_DK
    vd = RET_HEADS * RET_DV
    rowsel = lax.broadcasted_iota(jnp.int32, (nb, 1), 0) == b
    eye = (lax.broadcasted_iota(jnp.int32, (RET_DK, RET_DK), 0)
           == lax.broadcasted_iota(jnp.int32, (RET_DK, RET_DK), 1))

    def row(lo, width):
        return jnp.sum(jnp.where(rowsel, p_ref[:, lo:lo + width], 0.0), axis=0, keepdims=True)

    def column(r):
        return jnp.sum(jnp.where(eye, r, 0.0), axis=-1, keepdims=True)

    for h in range(RET_HEADS):
        gamma = 1.0 - 2.0 ** (-5.0 - h)
        q = row(h * RET_DK, RET_DK)
        k = row(qd + h * RET_DK, RET_DK)
        v = row(2 * qd + h * RET_DV, RET_DV)
        g = row(2 * qd + vd + h * RET_DV, RET_DV)
        s_old = s_ref[h]
        sn_ref[h] = gamma * s_old + column(k) * v
        qk = jnp.sum(q * k, axis=-1, keepdims=True)
        cross = jnp.sum(column(q) * s_old, axis=0, keepdims=True) * gamma
        o = qk * v + cross
        o = o * lax.rsqrt(jnp.mean(o * o, axis=-1, keepdims=True) + EPS)
        o_ref[:, h * RET_DV:(h + 1) * RET_DV] = o * _silu(g)


def _ret_sample_call(proj, s0):
    nb, n = proj.shape
    state = pl.BlockSpec((None, RET_HEADS, RET_DK, RET_DV), lambda b: (b, 0, 0, 0))
    return pl.pallas_call(
        _ret_sample_kernel,
        grid=(nb,),
        in_specs=[pl.BlockSpec((nb, n), lambda b: (0, 0)), state],
        out_specs=[pl.BlockSpec((None, 1, RET_HEADS * RET_DV), lambda b: (b, 0, 0)), state],
        out_shape=[jax.ShapeDtypeStruct((nb, 1, RET_HEADS * RET_DV), F32),
                   jax.ShapeDtypeStruct((nb, RET_HEADS, RET_DK, RET_DV), F32)],
        compiler_params=_cparams(1),
        name="retention_sample",
    )(proj, s0)


def _route(logits):
    tm = logits.shape[0]
    lane = lax.broadcasted_iota(jnp.int32, (tm, LANES), 1).astype(F32)
    ninf = -jnp.inf
    lg = jnp.where((lane >= N_EXPERTS) & (lane < N_EXPERTS + N_GROUPS), logits, ninf)
    mx = jnp.max(lg, axis=-1, keepdims=True)
    p_group = 1.0 / jnp.sum(jnp.exp(lg - mx), axis=-1, keepdims=True)
    gi = jnp.min(jnp.where(lg == mx, lane, float(LANES)), axis=-1, keepdims=True) - N_EXPERTS
    lo = gi * EXPERTS_PER_GROUP
    les = jnp.where((lane >= lo) & (lane < lo + EXPERTS_PER_GROUP), logits, ninf)
    v1 = jnp.max(les, axis=-1, keepdims=True)
    i1 = jnp.min(jnp.where(les == v1, lane, float(LANES)), axis=-1, keepdims=True)
    les2 = jnp.where(lane == i1, ninf, les)
    v2 = jnp.max(les2, axis=-1, keepdims=True)
    i2 = jnp.min(jnp.where(les2 == v2, lane, float(LANES)), axis=-1, keepdims=True)
    e2 = jnp.exp(v2 - v1)
    w1 = 1.0 / (1.0 + e2)
    w2 = e2 / (1.0 + e2)
    gates = jnp.where(lane == i1, w1 * p_group, 0.0) + jnp.where(lane == i2, w2 * p_group, 0.0)
    return gates + jnp.where(lane == GROUP_ID_LANE, gi, 0.0)


def _mixout_kernel(o_ref, w_ref, x_ref, g1_ref, sh_ref, sc_ref, gf_ref, wr_ref, x1_ref, h2_ref, gates_ref, *,
                   precise):
    y = _mm(o_ref[...], w_ref[...], precise)
    x1 = x_ref[...] + g1_ref[...] * y
    x1_ref[...] = x1
    h2 = _rms_mod(x1, gf_ref[...], sh_ref[...], sc_ref[...])
    h2_ref[...] = h2.astype(h2_ref.dtype)
    if precise:
        logits = _mm(h2, wr_ref[0], True)
    else:
        h_hi = h2.astype(BF16)
        h_lo = (h2 - h_hi.astype(F32)).astype(BF16)
        logits = _dot(h_hi, wr_ref[0]) + (_dot(h_lo, wr_ref[0]) + _dot(h_hi, wr_ref[1]))
    gates_ref[...] = _route(logits)


def _mixout_call(o, w_out, x, mod, layer, g_ffn, w_router, tm, precise):
    bx, lx, d = x.shape
    kd = o.shape[-1]
    return pl.pallas_call(
        functools.partial(_mixout_kernel, precise=precise),
        grid=(bx, lx // tm),
        in_specs=[
            _tok_spec(tm, kd),
            _const_spec((kd, d)),
            _tok_spec(tm, d),
            mod.spec(layer, 2), mod.spec(layer, 3), mod.spec(layer, 4),
            _const_spec((1, d)),
            _const_spec((2, d, LANES)),
        ],
        out_specs=[_tok_spec(tm, d), _tok_spec(tm, d), _tok_spec(tm, LANES)],
        out_shape=[jax.ShapeDtypeStruct((bx, lx, d), F32),
                   jax.ShapeDtypeStruct((bx, lx, d), F32 if precise else BF16),
                   jax.ShapeDtypeStruct((bx, lx, LANES), F32)],
        compiler_params=_cparams(2),
        name="mix_out_router",
    )(o, w_out, x, mod.arr, mod.arr, mod.arr, g_ffn, w_router)


def _moe_kernel(h_ref, gates_ref, wg_ref, wu_ref, wd_ref, x1_ref, g2_ref, gfin_ref, out_ref, acc_ref, *, final,
                precise):
    e = pl.program_id(2)

    @pl.when(e == 0)
    def _():
        acc_ref[...] = jnp.zeros_like(acc_ref)

    h = h_ref[...]
    a = _mm(h, wg_ref[...], precise)
    u = _mm(h, wu_ref[...], precise)
    lane = lax.broadcasted_iota(jnp.int32, gates_ref.shape, 1)
    ge = jnp.sum(jnp.where(lane == e, gates_ref[...], 0.0), axis=-1, keepdims=True)
    act = (_silu(a) * u) * ge
    acc_ref[...] += _mm(act, wd_ref[...], precise)

    @pl.when(e == pl.num_programs(2) - 1)
    def _():
        x2 = x1_ref[...] + g2_ref[...] * acc_ref[...]
        if final:
            x2 = (x2 * lax.rsqrt(jnp.mean(x2 * x2, axis=-1, keepdims=True) + EPS)) * gfin_ref[...]
        out_ref[...] = x2


def _moe_call(h2, gates, wg, wu, wd, layer, x1, mod, g_final, tm, final, precise):
    bx, lx, d = x1.shape
    tok3 = lambda w: pl.BlockSpec((None, tm, w), lambda b, i, e: (b, i, 0))
    g2_spec2 = mod.spec(layer, 5)
    g2_spec = pl.BlockSpec(g2_spec2.block_shape, lambda b, i, e: g2_spec2.index_map(b, i))
    return pl.pallas_call(
        functools.partial(_moe_kernel, final=final, precise=precise),
        grid=(bx, lx // tm, N_EXPERTS),
        in_specs=[
            tok3(d), tok3(LANES),
            pl.BlockSpec((None, d, D_EXPERT), lambda b, i, e: (layer, 0, e)),
            pl.BlockSpec((None, d, D_EXPERT), lambda b, i, e: (layer, 0, e)),
            pl.BlockSpec((None, D_EXPERT, d), lambda b, i, e: (layer, e, 0)),
            tok3(d),
            g2_spec,
            pl.BlockSpec((1, d), lambda b, i, e: (0, 0)),
        ],
        out_specs=tok3(d),
        out_shape=jax.ShapeDtypeStruct((bx, lx, d), F32),
        scratch_shapes=[pltpu.VMEM((tm, d), F32)],
        compiler_params=_cparams(3),
        name="hier_moe",
    )(h2, gates, wg, wu, wd, x1, mod.arr, g_final)


def _moe_plan_kernel(gates_ref, pos_ref, gs_ref, flags_ref, *, chunk, rows):
    g = gates_ref[...]
    tm = g.shape[0]
    lane = lax.broadcasted_iota(jnp.int32, (tm, LANES), 1)
    gid = jnp.sum(jnp.where(lane == GROUP_ID_LANE, g, 0.0), axis=-1, keepdims=True)
    onehot = jnp.where((lane.astype(F32) == gid) & (lane < N_GROUPS), 1.0, 0.0)
    r = lax.broadcasted_iota(jnp.int32, (tm, tm), 0)
    c = lax.broadcasted_iota(jnp.int32, (tm, tm), 1)
    before = _dot((c < r).astype(BF16), onehot.astype(BF16))
    rank = jnp.sum(onehot * before, axis=-1, keepdims=True)
    count = jnp.broadcast_to(jnp.sum(onehot, axis=0, keepdims=True), (8, LANES))
    lower = (lax.broadcasted_iota(jnp.int32, (LANES, LANES), 0)
             < lax.broadcasted_iota(jnp.int32, (LANES, LANES), 1)).astype(BF16)
    hi, mid, lo = _split3(jnp.ceil(count / chunk) * chunk)
    start = (_dot(hi, lower) + _dot(mid, lower)) + _dot(lo, lower)
    pos = jnp.sum(onehot * start[0:1], axis=-1, keepdims=True) + rank
    pos_ref[...] = jnp.broadcast_to(pos, (tm, LANES))
    dest = lax.broadcasted_iota(jnp.int32, (tm, rows), 1).astype(F32)
    place = (pos == dest).astype(BF16)
    hi, mid, lo = _split3(jnp.where(lane < N_EXPERTS, g, 0.0))
    gs = (_dot_tn(place, hi) + _dot_tn(place, mid)) + _dot_tn(place, lo)
    gs_ref[...] = gs
    flags_ref[...] = jnp.zeros_like(flags_ref)
    for k in range(rows // chunk):
        used = jnp.max(jnp.where(gs[k * chunk:(k + 1) * chunk] != 0.0, 1.0, 0.0), axis=0, keepdims=True)
        flags_ref[k:k + 1, :] = used.astype(jnp.int32)


def _moe_plan_call(gates, tm, chunk, rows):
    bx, lx, _ = gates.shape
    flag_rows = -(-(rows // chunk) // 8) * 8
    tok = lambda: pl.BlockSpec((None, tm, LANES), lambda b, i: (b, i, 0))
    per_tile = lambda r: pl.BlockSpec((None, None, r, LANES), lambda b, i: (b, i, 0, 0))
    return pl.pallas_call(
        functools.partial(_moe_plan_kernel, chunk=chunk, rows=rows),
        grid=(bx, lx // tm),
        in_specs=[tok()],
        out_specs=[tok(), per_tile(rows), per_tile(flag_rows)],
        out_shape=[jax.ShapeDtypeStruct((bx, lx, LANES), F32),
                   jax.ShapeDtypeStruct((bx, lx // tm, rows, LANES), F32),
                   jax.ShapeDtypeStruct((bx, lx // tm, flag_rows, LANES), jnp.int32)],
        compiler_params=_cparams(2),
        name="moe_plan",
    )(gates)


def _moe_sorted_kernel(first_ref, short_ref, nwin_ref, h_ref, pos_ref, gs_ref, wg_ref, wu_ref, wd_ref, x1_ref, g2_ref,
                       gfin_ref, out_ref, place_ref, xs_ref, gsp_ref, acc_ref, *, final, chunk, window, eps):
    b = pl.program_id(0)
    i = pl.program_id(1)
    step = pl.program_id(2)
    tm = h_ref.shape[0]
    rows_sorted = gs_ref.shape[0]
    tile = b * pl.num_programs(1) + i

    @pl.when(step == 0)
    def _():
        dest = lax.broadcasted_iota(jnp.int32, (tm, rows_sorted), 1).astype(F32)
        place = (pos_ref[:, 0:1] == dest).astype(BF16)
        place_ref[...] = place
        xs_ref[...] = jnp.zeros_like(xs_ref)
        xs_ref[0:rows_sorted, :] = _dot_tn(place, h_ref[...]).astype(BF16)
        gsp_ref[...] = jnp.zeros_like(gsp_ref)
        gsp_ref[0:rows_sorted, :] = gs_ref[...]
        acc_ref[...] = jnp.zeros_like(acc_ref)

    for j in range(eps):
        e = step * eps + j
        cols = slice(j * D_EXPERT, (j + 1) * D_EXPERT)

        def run(first_chunk, n_rows, e=e, cols=cols):
            rows = pl.ds(pl.multiple_of(first_chunk * chunk, chunk), n_rows)
            x = xs_ref[rows, :]
            lane = lax.broadcasted_iota(jnp.int32, (n_rows, LANES), 1)
            ge = jnp.sum(jnp.where(lane == e, gsp_ref[rows, :], 0.0), axis=-1, keepdims=True)
            act = (_silu(_dot(x, wg_ref[:, cols])) * _dot(x, wu_ref[:, cols])) * ge
            acc_ref[rows, :] += _dot(act.astype(BF16), wd_ref[cols, :])

        @pl.when(short_ref[tile, e] != 0)
        def _(run=run, e=e):
            run(first_ref[tile, e], (window - 1) * chunk)

        def one_window(w, carry, run=run, e=e):
            run(first_ref[tile, e] + w * window, window * chunk)
            return carry

        lax.fori_loop(0, nwin_ref[tile, e], one_window, 0)

    @pl.when(step == pl.num_programs(2) - 1)
    def _():
        x2 = x1_ref[...] + g2_ref[...] * _dot(place_ref[...], acc_ref[0:rows_sorted, :].astype(BF16))
        if final:
            x2 = (x2 * lax.rsqrt(jnp.mean(x2 * x2, axis=-1, keepdims=True) + EPS)) * gfin_ref[...]
        out_ref[...] = x2


def _moe_sorted_call(h2, gates, wg, wu, wd, layer, x1, mod, g_final, tm, final):
    bx, lx, d = x1.shape
    chunk = min(tm, MOE_CHUNK)
    window = MOE_WINDOW
    rows = tm + N_GROUPS * chunk
    n_chunks = rows // chunk
    pos, gs, flags = _moe_plan_call(gates, tm, chunk, rows)
    used = flags[:, :, :n_chunks, :N_EXPERTS].reshape(-1, n_chunks, N_EXPERTS) != 0
    idx = jnp.arange(n_chunks, dtype=jnp.int32)[None, :, None]
    first = jnp.min(jnp.where(used, idx, n_chunks), axis=1)
    last = jnp.max(jnp.where(used, idx, -1), axis=1)
    span = jnp.where(last >= 0, last - first + 1, 0)
    short = ((span > 0) & (span < window)).astype(jnp.int32)
    nwin = jnp.where(span >= window, (span + window - 1) // window, 0).astype(jnp.int32)
    first = jnp.where(last >= 0, first, 0).astype(jnp.int32)
    pad_rows = (window - 1) * chunk
    tok3 = lambda w: pl.BlockSpec((None, tm, w), lambda b, i, e, *_: (b, i, 0))
    g2_spec2 = mod.spec(layer, 5)
    g2_spec = pl.BlockSpec(g2_spec2.block_shape, lambda b, i, e, *_: g2_spec2.index_map(b, i))
    eps = MOE_EXPERTS_PER_STEP
    grid_spec = pltpu.PrefetchScalarGridSpec(
        num_scalar_prefetch=3,
        grid=(bx, lx // tm, N_EXPERTS // eps),
        in_specs=[
            tok3(d), tok3(LANES),
            pl.BlockSpec((None, None, rows, LANES), lambda b, i, e, *_: (b, i, 0, 0)),
            pl.BlockSpec((None, d, eps * D_EXPERT), lambda b, i, e, *_: (layer, 0, e)),
            pl.BlockSpec((None, d, eps * D_EXPERT), lambda b, i, e, *_: (layer, 0, e)),
            pl.BlockSpec((None, eps * D_EXPERT, d), lambda b, i, e, *_: (layer, e, 0)),
            tok3(d),
            g2_spec,
            pl.BlockSpec((1, d), lambda b, i, e, *_: (0, 0)),
        ],
        out_specs=tok3(d),
        scratch_shapes=[pltpu.VMEM((tm, rows), BF16), pltpu.VMEM((rows + pad_rows, d), BF16),
                        pltpu.VMEM((rows + pad_rows, LANES), F32), pltpu.VMEM((rows + pad_rows, d), F32)],
    )
    return pl.pallas_call(
        functools.partial(_moe_sorted_kernel, final=final, chunk=chunk, window=window, eps=eps),
        grid_spec=grid_spec,
        out_shape=jax.ShapeDtypeStruct((bx, lx, d), F32),
        compiler_params=_cparams(3),
        name="hier_moe_sorted",
    )(first, short, nwin, h2, pos, gs, wg, wu, wd, x1, mod.arr, g_final)


def _log_sigmoid(z):
    return -(jnp.maximum(-z, 0.0) + jnp.log1p(jnp.exp(-jnp.abs(z))))


def _kvq_kernel(x_ref, shk_ref, sck_ref, gkv_ref, wkv_ref, wf_ref, bf_ref, sh1_ref, sc1_ref, gmix_ref, wq_ref,
                k32_ref, v32_ref, k16_ref, v16_ref, lf_ref, q_ref, *, precise, q_scale):
    x = x_ref[...]
    fd = FOX_HEADS * FOX_HD
    n = _rms_mod(x, gkv_ref[...], shk_ref[...], sck_ref[...])
    if not precise:
        n = n.astype(BF16)
    k = _mm(n, wkv_ref[:, 0:fd], precise)
    k32_ref[...] = k
    k16_ref[...] = k.astype(BF16)
    v = _mm(n, wkv_ref[:, fd:2 * fd], precise)
    v32_ref[...] = v
    v16_ref[...] = v.astype(BF16)
    z = _mm(n, wf_ref[...], precise) + bf_ref[...]
    lf_ref[...] = _log_sigmoid(z)[:, :FOX_HEADS]
    h = _rms_mod(x, gmix_ref[...], sh1_ref[...], sc1_ref[...])
    q_ref[...] = (_mm(h, wq_ref[...], precise) * q_scale).astype(q_ref.dtype)


def _kvq_call(x, kvmod, mod, g_kv, w_kv, w_f, b_f, g_mix, w_q, tm, precise, q_scale):
    bx, lx, d = x.shape
    fd = FOX_HEADS * FOX_HD
    return pl.pallas_call(
        functools.partial(_kvq_kernel, precise=precise, q_scale=q_scale),
        grid=(bx, lx // tm),
        in_specs=[
            _tok_spec(tm, d),
            kvmod.spec(0, 0), kvmod.spec(0, 1),
            _const_spec((1, d)),
            _const_spec((d, 2 * fd)),
            _const_spec((d, LANES)),
            _const_spec((1, LANES)),
            mod.spec(1, 0), mod.spec(1, 1),
            _const_spec((1, d)),
            _const_spec((d, fd)),
        ],
        out_specs=[_tok_spec(tm, fd), _tok_spec(tm, fd), _tok_spec(tm, fd), _tok_spec(tm, fd),
                   _tok_spec(tm, FOX_HEADS), _tok_spec(tm, fd)],
        out_shape=[jax.ShapeDtypeStruct((bx, lx, fd), F32), jax.ShapeDtypeStruct((bx, lx, fd), F32),
                   jax.ShapeDtypeStruct((bx, lx, fd), BF16), jax.ShapeDtypeStruct((bx, lx, fd), BF16),
                   jax.ShapeDtypeStruct((bx, lx, FOX_HEADS), F32),
                   jax.ShapeDtypeStruct((bx, lx, fd), F32 if precise else BF16)],
        compiler_params=_cparams(2),
        name="kv_q_proj",
    )(x, kvmod.arr, kvmod.arr, g_kv, w_kv, w_f, b_f, mod.arr, mod.arr, g_mix, w_q)


def _cumsum_kernel(x_ref, o_ref):
    nh, l = x_ref.shape
    r = lax.broadcasted_iota(jnp.int32, (LANES, LANES), 0)
    c = lax.broadcasted_iota(jnp.int32, (LANES, LANES), 1)
    upper = (r <= c).astype(BF16)
    carry = jnp.zeros((nh, 1), F32)
    for blk in range(l // LANES):
        hi, mid, lo = _split3(x_ref[:, blk * LANES:(blk + 1) * LANES])
        cs = (_dot(hi, upper) + _dot(mid, upper)) + _dot(lo, upper) + carry
        o_ref[:, blk * LANES:(blk + 1) * LANES] = cs
        carry = cs[:, LANES - 1:LANES]


def _cumsum_call(x):
    b, nh, l = x.shape
    return pl.pallas_call(
        _cumsum_kernel,
        grid=(b,),
        in_specs=[pl.BlockSpec((None, nh, l), lambda i: (i, 0, 0))],
        out_specs=pl.BlockSpec((None, nh, l), lambda i: (i, 0, 0)),
        out_shape=jax.ShapeDtypeStruct((b, nh, l), F32),
        compiler_params=_cparams(1),
        name="logf_cumsum",
    )(x)


def _fox_kernel(q_ref, k_ref, v_ref, fq_ref, fk_ref, o_ref, acc_ref, *, t):
    qi = pl.program_id(2)
    first = lax.broadcasted_iota(jnp.int32, (1, LANES), 1) < FOX_HD
    col_a = lax.broadcasted_iota(jnp.int32, (1, 2 * t), 1) < t
    q2 = q_ref[...]
    zero = jnp.zeros_like(q2)
    qs = jnp.concatenate([jnp.where(first, q2, zero), jnp.where(first, zero, q2)], axis=0)
    fq = jnp.concatenate([fq_ref[0:1, :], fq_ref[1:2, :]], axis=1) * LOG2E
    key = lax.broadcasted_iota(jnp.int32, (t, 2 * t), 0)
    qry = lax.broadcasted_iota(jnp.int32, (t, 2 * t), 1)
    causal = key <= jnp.where(col_a, qry, qry - t)
    acc_ref[...] = jnp.zeros_like(acc_ref)

    def tile(j, m, l, diagonal):
        start = pl.multiple_of(j * t, t)
        kt = k_ref[pl.ds(start, t), :]
        vt = v_ref[pl.ds(start, t), :]
        fk = fk_ref[pl.ds(start, t), :] * LOG2E
        s = (_dot_nt(kt, qs) + fq) - jnp.where(col_a, fk[:, 0:1], fk[:, 1:2])
        if diagonal:
            s = jnp.where(causal, s, NEG_INF)
        m_new = jnp.maximum(m, jnp.max(s, axis=0, keepdims=True))
        p = jnp.exp2(s - m_new)
        alpha = jnp.exp2(m - m_new)
        acc_ref[...] = alpha * acc_ref[...] + _dot_tn(vt, p.astype(BF16))
        return m_new, alpha * l + jnp.sum(p, axis=0, keepdims=True)

    m0 = jnp.full((1, 2 * t), -jnp.inf, F32)
    l0 = jnp.zeros((1, 2 * t), F32)
    m, l = lax.fori_loop(0, qi, lambda j, c: tile(j, c[0], c[1], False), (m0, l0))
    m, l = tile(qi, m, l, True)
    o_t = acc_ref[...] / l
    row_a = lax.broadcasted_iota(jnp.int32, (LANES, 1), 0) < FOX_HD
    o_ref[...] = jnp.where(row_a, o_t[:, :t], o_t[:, t:]).T.astype(BF16)


def _fox_prompt_call(q, k16, v16, f_rows, f_cols, tq):
    b, l, fd = q.shape
    npair = fd // LANES
    return pl.pallas_call(
        functools.partial(_fox_kernel, t=tq),
        grid=(b, npair, l // tq),
        in_specs=[
            pl.BlockSpec((None, tq, LANES), lambda bb, hp, i: (bb, i, hp)),
            pl.BlockSpec((None, l, LANES), lambda bb, hp, i: (bb, 0, hp)),
            pl.BlockSpec((None, l, LANES), lambda bb, hp, i: (bb, 0, hp)),
            pl.BlockSpec((None, None, None, 2, tq), lambda bb, hp, i: (bb, hp, i, 0, 0)),
            pl.BlockSpec((None, None, l, 2), lambda bb, hp, i: (bb, hp, 0, 0)),
        ],
        out_specs=pl.BlockSpec((None, tq, LANES), lambda bb, hp, i: (bb, i, hp)),
        out_shape=jax.ShapeDtypeStruct((b, l, fd), BF16),
        scratch_shapes=[pltpu.VMEM((LANES, 2 * tq), F32)],
        compiler_params=_cparams(3),
        name="fox_attention_prompt",
    )(q, k16, v16, f_rows, f_cols)


def _attn_sample_kernel(pt_ref, q_ref, kn_ref, vn_ref, lfn_ref, *refs, pps, page):
    ck = refs[0:pps]
    cv = refs[pps:2 * pps]
    clf = refs[2 * pps:3 * pps]
    o_ref = refs[3 * pps]
    m_ref, l_ref, acc_ref, carry_ref = refs[3 * pps + 1:]
    step = pl.program_id(1)
    nh, hd = FOX_HEADS, FOX_HD
    fd = nh * hd
    diag = (lax.broadcasted_iota(jnp.int32, (nh, fd), 1) // hd) == lax.broadcasted_iota(jnp.int32, (nh, fd), 0)
    q_bd = jnp.where(diag, jnp.broadcast_to(q_ref[...], (nh, fd)), 0.0)

    @pl.when(step == 0)
    def _():
        m_ref[...] = jnp.sum(q_bd * kn_ref[...], axis=-1, keepdims=True)
        l_ref[...] = jnp.ones_like(l_ref)
        acc_ref[...] = jnp.broadcast_to(vn_ref[...], (nh, fd))
        carry_ref[...] = lfn_ref[...]

    tt = lax.broadcasted_iota(jnp.int32, (page, 2 * page), 0)
    cc = lax.broadcasted_iota(jnp.int32, (page, 2 * page), 1)
    later = ((tt > cc) | (cc >= page)).astype(BF16)

    def split2(x):
        hi = x.astype(BF16)
        return jnp.concatenate([hi, (x - hi.astype(F32)).astype(BF16)], axis=0)

    q2 = split2(q_bd)
    lf_parts = [part for i in range(pps) for part in _split3(clf[i][...])]
    lfx_all = _dot(jnp.concatenate(lf_parts, axis=0), later)
    carry = carry_ref[...]
    scores = []
    for i in range(pps):
        base = 3 * nh * i
        lfx = (lfx_all[base:base + nh] + lfx_all[base + nh:base + 2 * nh]) + lfx_all[base + 2 * nh:base + 3 * nh]
        bias = lfx[:, :page] + carry
        carry = carry + lfx[:, page:page + 1]
        s2 = _dot(q2, ck[i][...].reshape(fd, page).astype(BF16))
        scores.append((s2[:nh] + s2[nh:]) + bias)
    carry_ref[...] = carry
    s = jnp.concatenate(scores, axis=1)
    m_old = m_ref[...]
    m_new = jnp.maximum(m_old, jnp.max(s, axis=-1, keepdims=True))
    p = jnp.exp(s - m_new)
    alpha = jnp.exp(m_old - m_new)
    m_ref[...] = m_new
    l_ref[...] = alpha * l_ref[...] + jnp.sum(p, axis=-1, keepdims=True)
    p2 = split2(p)
    pv = _dot_nt(p2[:, 0:page], cv[0][...].reshape(fd, page).astype(BF16))
    for i in range(1, pps):
        pv = pv + _dot_nt(p2[:, i * page:(i + 1) * page], cv[i][...].reshape(fd, page).astype(BF16))
    acc_ref[...] = alpha * acc_ref[...] + (pv[:nh] + pv[nh:])

    @pl.when(step == pl.num_programs(1) - 1)
    def _():
        o = acc_ref[...] / l_ref[...]
        o_ref[...] = jnp.sum(jnp.where(diag, o, 0.0), axis=0, keepdims=True)


def _attn_sample_call(page_table, q, k_new, v_new, lf_new, cache_kt, cache_vt, cache_lft, pps):
    nb, n_pages = page_table.shape
    n_pool, nh, hd, page = cache_kt.shape
    fd = nh * hd
    steps = n_pages // pps

    def page_map(i, nd):
        return lambda b, s, pt: (pt[b, n_pages - 1 - (s * pps + i)],) + (0,) * nd

    row = lambda r, w: pl.BlockSpec((None, r, w), lambda b, s, pt: (b, 0, 0))
    in_specs = [row(1, fd), row(1, fd), row(1, fd), row(nh, 1)]
    in_specs += [pl.BlockSpec((None, nh, hd, page), page_map(i, 3)) for i in range(pps)]
    in_specs += [pl.BlockSpec((None, nh, hd, page), page_map(i, 3)) for i in range(pps)]
    in_specs += [pl.BlockSpec((None, nh, page), page_map(i, 2)) for i in range(pps)]
    grid_spec = pltpu.PrefetchScalarGridSpec(
        num_scalar_prefetch=1,
        grid=(nb, steps),
        in_specs=in_specs,
        out_specs=row(1, fd),
        scratch_shapes=[pltpu.VMEM((nh, 1), F32), pltpu.VMEM((nh, 1), F32), pltpu.VMEM((nh, fd), F32),
                        pltpu.VMEM((nh, 1), F32)],
    )
    return pl.pallas_call(
        functools.partial(_attn_sample_kernel, pps=pps, page=page),
        grid_spec=grid_spec,
        out_shape=jax.ShapeDtypeStruct((nb, 1, fd), F32),
        compiler_params=_cparams(2),
        name="fox_attention_sample",
    )(page_table, q, k_new, v_new, lf_new, *([cache_kt] * pps), *([cache_vt] * pps), *([cache_lft] * pps))


def _rope_tables(pos):
    half = RET_DK // 2
    inv = ROPE_BASE ** (-jnp.arange(half, dtype=F32) / half)
    ang = pos.astype(F32)[:, None] * inv[None, :]
    return jnp.cos(ang), jnp.sin(ang)


def _pick_tile(n, pref):
    t = min(n, pref)
    while n % t:
        t //= 2
    return t


def kernel(x_prompt, x_sample, c_prompt, c_sample, state_ret, cache_k, cache_v, cache_logf, page_table, w_mod, b_mod, g_mix, g_ffn, w_ret_in, w_ret_out, g_kv, w_kvmod, b_kvmod, w_kv, w_f, b_f, w_fq, w_fo, w_rg, w_re, w_e_gate, w_e_up, w_e_down, g_final):
    b, l, d = x_prompt.shape
    nb = x_sample.shape[0]
    n_pool, page = cache_k.shape[0], cache_k.shape[1]
    n_pages = page_table.shape[1]
    past_len = n_pages * page
    fd = FOX_HEADS * FOX_HD

    c_all = jnp.concatenate([c_sample, c_prompt], axis=0)
    mod_all = _mod_call(c_all, w_mod, b_mod, 1536)
    kvmod_all = _mod_call(c_all, w_kvmod[None], b_kvmod[None], 1024)

    w_f_pad = jnp.pad(w_f, ((0, 0), (0, LANES - FOX_HEADS)))
    weights32 = dict(w_in=w_ret_in, w_out=w_ret_out[0], w_kv=w_kv, w_fq=w_fq[0], w_fo=w_fo[0], w_f=w_f_pad,
                     wg=w_e_gate, wu=w_e_up, wd=w_e_down)
    weights16 = {name: w.astype(BF16) for name, w in weights32.items()}
    b_f_row = jnp.pad(b_f, (0, LANES - FOX_HEADS)).reshape(1, LANES)
    w_r = jnp.pad(jnp.concatenate([w_re, w_rg], axis=-1), ((0, 0), (0, 0), (0, LANES - N_EXPERTS - N_GROUPS)))
    w_r_hi = w_r.astype(BF16)
    routers32 = jnp.stack([w_r, jnp.zeros_like(w_r)], axis=1)
    routers16 = jnp.stack([w_r_hi, (w_r - w_r_hi.astype(F32)).astype(BF16)], axis=1)
    g_mix2 = g_mix.reshape(-1, 1, d)
    g_ffn2 = g_ffn.reshape(-1, 1, d)
    g_kv2 = g_kv.reshape(1, d)
    g_fin2 = g_final.reshape(1, d)

    def trunk(x, mod, kvmod, pos, tm, tm_in, tm_moe, mixer0, mixer1, precise, q_scale):
        w = weights32 if precise else weights16
        cos, sin = _rope_tables(pos)
        proj = _retin_call(x, mod, g_mix2[0], w["w_in"], cos, sin, tm_in, precise)
        o, s_new = mixer0(proj)
        def moe(h, gates, layer, x_res, final):
            if precise:
                return _moe_call(h, gates, w["wg"], w["wu"], w["wd"], layer, x_res, mod, g_fin2, tm_moe, final, True)
            return _moe_sorted_call(h, gates, w["wg"], w["wu"], w["wd"], layer, x_res, mod, g_fin2, tm_moe, final)

        w_router = routers32 if precise else routers16
        x1, h2, gates = _mixout_call(o, w["w_out"], x, mod, 0, g_ffn2[0], w_router[0], tm, precise)
        x2 = moe(h2, gates, 0, x1, False)
        k32, v32, k16, v16, lf, qf = _kvq_call(x2, kvmod, mod, g_kv2, w["w_kv"], w["w_f"], b_f_row, g_mix2[1],
                                               w["w_fq"], tm, precise, q_scale)
        o1 = mixer1(qf, k32, v32, k16, v16, lf)
        x3, h4, gates1 = _mixout_call(o1, w["w_fo"], x2, mod, 1, g_ffn2[1], w_router[1], tm, precise)
        y = moe(h4, gates1, 1, x3, True)
        return y, s_new, k32, v32, lf

    tm_p = _pick_tile(l, 512)
    chunk = _pick_tile(l, 256)
    tq = _pick_tile(l, 512)

    def ret_prompt(proj):
        return _ret_prompt_call(proj, chunk)

    def fox_prompt(qf, k32, v32, k16, v16, lf):
        f_t = _cumsum_call(jnp.swapaxes(lf, 1, 2))
        f_pairs = f_t.reshape(b, FOX_HEADS // 2, 2, l)
        f_cols = jnp.swapaxes(f_pairs, 2, 3)
        f_rows = jnp.swapaxes(f_pairs.reshape(b, FOX_HEADS // 2, 2, l // tq, tq), 2, 3)
        return _fox_prompt_call(qf, k16, v16, f_rows, f_cols, tq)

    mod_p = _Mod(mod_all, False, nb, tm_p)
    kvmod_p = _Mod(kvmod_all, False, nb, tm_p)
    y_p, s_p, k_p, v_p, lf_p = trunk(x_prompt, mod_p, kvmod_p, jnp.arange(l), tm_p, _pick_tile(l, 1024),
                                     _pick_tile(l, 1024), ret_prompt, fox_prompt, False, FOX_HD ** -0.5 * LOG2E)

    def ret_sample(proj):
        o, s_new = _ret_sample_call(proj[0], state_ret[0])
        return o.reshape(1, nb, -1), s_new

    def fox_sample(qf, k32, v32, k16, v16, lf):
        rows = lambda a: a.reshape(nb, 1, fd)
        o = _attn_sample_call(page_table, rows(qf), rows(k32), rows(v32), lf.reshape(nb, FOX_HEADS, 1),
                              jnp.transpose(cache_k, (0, 2, 3, 1)), jnp.transpose(cache_v, (0, 2, 3, 1)),
                              jnp.swapaxes(cache_logf, 1, 2), _pick_tile(n_pages, 8))
        return o.reshape(1, nb, fd)

    mod_s = _Mod(mod_all, True, 0, nb)
    kvmod_s = _Mod(kvmod_all, True, 0, nb)
    y_s, s_s, k_s, v_s, lf_s = trunk(x_sample.reshape(1, nb, d), mod_s, kvmod_s,
                                     jnp.full((nb,), past_len, jnp.int32), nb, nb, nb, ret_sample, fox_sample, True,
                                     FOX_HD ** -0.5)

    return (y_p, y_s.reshape(nb, 1, d), s_p[None], s_s[None],
            k_p.reshape(b, l, FOX_HEADS, FOX_HD), v_p.reshape(b, l, FOX_HEADS, FOX_HD), lf_p,
            k_s.reshape(nb, 1, FOX_HEADS, FOX_HD), v_s.reshape(nb, 1, FOX_HEADS, FOX_HD),
            lf_s.reshape(nb, 1, FOX_HEADS))
```

```python
import functools
import math

import jax
import jax.numpy as jnp
from jax import lax
from jax.experimental import pallas as pl
from jax.experimental.pallas import tpu as pltpu

D_MODEL = 1024
RET_HEADS = 4
RET_DK = 256
RET_DV = 512
ROPE_BASE = 10000.0
FOX_HEADS = 16
FOX_HD = 64
N_GROUPS = 4
EXPERTS_PER_GROUP = 8
N_EXPERTS = 32
D_EXPERT = 256
EPS = 1e-6
NEG_INF = -1e30
LOG2E = 1.4426950408889634

LANES = 128
VMEM_LIMIT = 56 * 1024 * 1024
GROUP_ID_LANE = 64
MOE_CHUNK = 128
MOE_WINDOW = 3
MOE_EXPERTS_PER_STEP = 4
F32 = jnp.float32
BF16 = jnp.bfloat16


def _cparams(n_axes):
    return pltpu.CompilerParams(dimension_semantics=("arbitrary",) * n_axes,
                                vmem_limit_bytes=VMEM_LIMIT)


def _silu(x):
    return x * jax.nn.sigmoid(x)


def _dot(a, b):
    return jnp.dot(a, b, preferred_element_type=F32)


def _dot_nt(a, b):
    return lax.dot_general(a, b, (((1,), (1,)), ((), ())), preferred_element_type=F32)


def _dot_tn(a, b):
    return lax.dot_general(a, b, (((0,), (0,)), ((), ())), preferred_element_type=F32)


def _mm(a, w, precise):
    if precise:
        a_hi = a.astype(BF16)
        a_lo = (a - a_hi.astype(F32)).astype(BF16)
        w_hi = w.astype(BF16)
        w_lo = (w - w_hi.astype(F32)).astype(BF16)
        return _dot(a_hi, w_hi) + (_dot(a_lo, w_hi) + _dot(a_hi, w_lo))
    return _dot(a.astype(BF16), w)


def _split3(x):
    hi = x.astype(BF16)
    r1 = x - hi.astype(F32)
    mid = r1.astype(BF16)
    lo = (r1 - mid.astype(F32)).astype(BF16)
    return hi, mid, lo


def _rms_mod(x, g, shift, scale):
    y = x * lax.rsqrt(jnp.mean(x * x, axis=-1, keepdims=True) + EPS)
    return (y * g) * (1.0 + scale) + shift


def _mod_kernel(c_ref, w_ref, b_ref, o_ref):
    o_ref[...] = _mm(_silu(c_ref[...]), w_ref[...], True) + b_ref[...]


def _mod_call(c, w, b, tn):
    ns, k, n = w.shape
    m = c.shape[0]
    return pl.pallas_call(
        _mod_kernel,
        grid=(ns, n // tn),
        in_specs=[
            pl.BlockSpec((m, k), lambda s, j: (0, 0)),
            pl.BlockSpec((None, k, tn), lambda s, j: (s, 0, j)),
            pl.BlockSpec((None, 1, tn), lambda s, j: (s, 0, j)),
        ],
        out_specs=pl.BlockSpec((None, m, tn), lambda s, j: (s, 0, j)),
        out_shape=jax.ShapeDtypeStruct((ns, m, n), F32),
        compiler_params=_cparams(2),
        name="adaln_mod",
    )(c, w, b.reshape(ns, 1, n))


class _Mod:
    def __init__(self, arr, per_token, row0, tm):
        self.per_token = per_token
        self.row0 = row0
        self.tm = tm
        self.arr = arr if per_token else arr.reshape(arr.shape[0], arr.shape[1], 1, arr.shape[2])

    def spec(self, stack, col):
        if self.per_token:
            assert self.row0 % self.tm == 0
            r0 = self.row0 // self.tm
            return pl.BlockSpec((None, self.tm, D_MODEL), lambda b, i: (stack, r0 + i, col))
        row0 = self.row0
        return pl.BlockSpec((None, None, 1, D_MODEL), lambda b, i: (stack, row0 + b, 0, col))


def _tok_spec(tm, width):
    return pl.BlockSpec((None, tm, width), lambda b, i: (b, i, 0))


def _const_spec(shape):
    nd = len(shape)
    return pl.BlockSpec(shape, lambda b, i: (0,) * nd)


def _retin_kernel(x_ref, sh_ref, sc_ref, g_ref, w_ref, cos_ref, sin_ref, o_ref, h_ref, *, precise):
    j = pl.program_id(2)

    @pl.when(j == 0)
    def _():
        h_ref[...] = _rms_mod(x_ref[...], g_ref[...], sh_ref[...], sc_ref[...]).astype(h_ref.dtype)

    p = _mm(h_ref[...], w_ref[...], precise)

    @pl.when(j < 2)
    def _():
        scale = jnp.where(j == 0, 1.0, RET_DK ** -0.5)
        half = RET_DK // 2
        cos = cos_ref[...]
        sin = sin_ref[...]
        for hh in range(RET_HEADS):
            a = hh * RET_DK
            x1 = p[:, a:a + half]
            x2 = p[:, a + half:a + RET_DK]
            o_ref[:, a:a + half] = ((x1 * cos - x2 * sin) * scale).astype(o_ref.dtype)
            o_ref[:, a + half:a + RET_DK] = ((x1 * sin + x2 * cos) * scale).astype(o_ref.dtype)

    @pl.when(j >= 2)
    def _():
        o_ref[...] = p.astype(o_ref.dtype)


def _retin_call(x, mod, g_mix, w_in, cos, sin, tm, precise):
    bx, lx, d = x.shape
    tn = RET_HEADS * RET_DK
    n = w_in.shape[-1]
    dt = F32 if precise else BF16
    spec3 = lambda sp: pl.BlockSpec(sp.block_shape, lambda b, i, j: sp.index_map(b, i))
    return pl.pallas_call(
        functools.partial(_retin_kernel, precise=precise),
        grid=(bx, lx // tm, n // tn),
        in_specs=[
            spec3(_tok_spec(tm, d)),
            spec3(mod.spec(0, 0)), spec3(mod.spec(0, 1)),
            spec3(_const_spec((1, d))),
            pl.BlockSpec((None, d, tn), lambda b, i, j: (0, 0, j)),
            pl.BlockSpec((tm, RET_DK // 2), lambda b, i, j: (i, 0)),
            pl.BlockSpec((tm, RET_DK // 2), lambda b, i, j: (i, 0)),
        ],
        out_specs=pl.BlockSpec((None, tm, tn), lambda b, i, j: (b, i, j)),
        out_shape=jax.ShapeDtypeStruct((bx, lx, n), dt),
        scratch_shapes=[pltpu.VMEM((tm, d), dt)],
        compiler_params=_cparams(3),
        name="ret_in_proj",
    )(x, mod.arr, mod.arr, g_mix, w_in, cos, sin)


def _log_gamma(h):
    return math.log(1.0 - 2.0 ** (-5.0 - h))


def _ret_prompt_kernel(q_ref, k_ref, v_ref, g_ref, o_ref, s_ref, *, chunk):
    @pl.when(pl.program_id(1) == 0)
    def _():
        s_ref[...] = jnp.zeros_like(s_ref)

    ti = lax.broadcasted_iota(jnp.int32, (chunk, chunk), 0)
    tj = lax.broadcasted_iota(jnp.int32, (chunk, chunk), 1)
    rel = (ti - tj).astype(F32)
    t = lax.broadcasted_iota(jnp.int32, (chunk, 1), 0).astype(F32)
    for h in range(RET_HEADS):
        lg = _log_gamma(h)
        decay = jnp.where(rel >= 0, jnp.exp(lg * jnp.maximum(rel, 0.0)), 0.0)
        qh = q_ref[:, h * RET_DK:(h + 1) * RET_DK]
        kh = k_ref[:, h * RET_DK:(h + 1) * RET_DK]
        vh = v_ref[:, h * RET_DV:(h + 1) * RET_DV]
        a = _dot_nt(qh, kh) * decay
        inner = _dot(a.astype(BF16), vh)
        s_old = s_ref[h]
        cross = _dot(qh, s_old.astype(BF16)) * jnp.exp((t + 1.0) * lg)
        o = inner + cross
        kd = (kh.astype(F32) * jnp.exp((chunk - 1.0 - t) * lg)).astype(BF16)
        s_ref[h] = math.exp(chunk * lg) * s_old + _dot_tn(kd, vh)
        o = o * lax.rsqrt(jnp.mean(o * o, axis=-1, keepdims=True) + EPS)
        gh = g_ref[:, h * RET_DV:(h + 1) * RET_DV].astype(F32)
        o_ref[:, h * RET_DV:(h + 1) * RET_DV] = (o * _silu(gh)).astype(BF16)


def _ret_prompt_call(proj, chunk):
    b, l, _ = proj.shape
    qd = RET_HEADS * RET_DK
    vd = RET_HEADS * RET_DV
    col = lambda w, j: pl.BlockSpec((None, chunk, w), lambda bb, c: (bb, c, j))
    return pl.pallas_call(
        functools.partial(_ret_prompt_kernel, chunk=chunk),
        grid=(b, l // chunk),
        in_specs=[col(qd, 0), col(qd, 1), col(vd, 1), col(vd, 2)],
        out_specs=[_tok_spec(chunk, vd),
                   pl.BlockSpec((None, RET_HEADS, RET_DK, RET_DV), lambda bb, c: (bb, 0, 0, 0))],
        out_shape=[jax.ShapeDtypeStruct((b, l, vd), BF16),
                   jax.ShapeDtypeStruct((b, RET_HEADS, RET_DK, RET_DV), F32)],
        compiler_params=_cparams(2),
        name="retention_prompt",
    )(proj, proj, proj, proj)


def _ret_sample_kernel(p_ref, s_ref, o_ref, sn_ref):
    b = pl.program_id(0)
    nb = p_ref.shape[0]
    qd = RET_HEADS * RET_DK
    vd = RET_HEADS * RET_DV
    rowsel = lax.broadcasted_iota(jnp.int32, (nb, 1), 0) == b
    eye = (lax.broadcasted_iota(jnp.int32, (RET_DK, RET_DK), 0)
           == lax.broadcasted_iota(jnp.int32, (RET_DK, RET_DK), 1))

    def row(lo, width):
        return jnp.sum(jnp.where(rowsel, p_ref[:, lo:lo + width], 0.0), axis=0, keepdims=True)

    def column(r):
        return jnp.sum(jnp.where(eye, r, 0.0), axis=-1, keepdims=True)

    for h in range(RET_HEADS):
        gamma = 1.0 - 2.0 ** (-5.0 - h)
        q = row(h * RET_DK, RET_DK)
        k = row(qd + h * RET_DK, RET_DK)
        v = row(2 * qd + h * RET_DV, RET_DV)
        g = row(2 * qd + vd + h * RET_DV, RET_DV)
        s_old = s_ref[h]
        sn_ref[h] = gamma * s_old + column(k) * v
        qk = jnp.sum(q * k, axis=-1, keepdims=True)
        cross = jnp.sum(column(q) * s_old, axis=0, keepdims=True) * gamma
        o = qk * v + cross
        o = o * lax.rsqrt(jnp.mean(o * o, axis=-1, keepdims=True) + EPS)
        o_ref[:, h * RET_DV:(h + 1) * RET_DV] = o * _silu(g)


def _ret_sample_call(proj, s0):
    nb, n = proj.shape
    state = pl.BlockSpec((None, RET_HEADS, RET_DK, RET_DV), lambda b: (b, 0, 0, 0))
    return pl.pallas_call(
        _ret_sample_kernel,
        grid=(nb,),
        in_specs=[pl.BlockSpec((nb, n), lambda b: (0, 0)), state],
        out_specs=[pl.BlockSpec((None, 1, RET_HEADS * RET_DV), lambda b: (b, 0, 0)), state],
        out_shape=[jax.ShapeDtypeStruct((nb, 1, RET_HEADS * RET_DV), F32),
                   jax.ShapeDtypeStruct((nb, RET_HEADS, RET_DK, RET_DV), F32)],
        compiler_params=_cparams(1),
        name="retention_sample",
    )(proj, s0)


def _route(logits):
    tm = logits.shape[0]
    lane = lax.broadcasted_iota(jnp.int32, (tm, LANES), 1).astype(F32)
    ninf = -jnp.inf
    lg = jnp.where((lane >= N_EXPERTS) & (lane < N_EXPERTS + N_GROUPS), logits, ninf)
    mx = jnp.max(lg, axis=-1, keepdims=True)
    p_group = 1.0 / jnp.sum(jnp.exp(lg - mx), axis=-1, keepdims=True)
    gi = jnp.min(jnp.where(lg == mx, lane, float(LANES)), axis=-1, keepdims=True) - N_EXPERTS
    lo = gi * EXPERTS_PER_GROUP
    les = jnp.where((lane >= lo) & (lane < lo + EXPERTS_PER_GROUP), logits, ninf)
    v1 = jnp.max(les, axis=-1, keepdims=True)
    i1 = jnp.min(jnp.where(les == v1, lane, float(LANES)), axis=-1, keepdims=True)
    les2 = jnp.where(lane == i1, ninf, les)
    v2 = jnp.max(les2, axis=-1, keepdims=True)
    i2 = jnp.min(jnp.where(les2 == v2, lane, float(LANES)), axis=-1, keepdims=True)
    e2 = jnp.exp(v2 - v1)
    w1 = 1.0 / (1.0 + e2)
    w2 = e2 / (1.0 + e2)
    gates = jnp.where(lane == i1, w1 * p_group, 0.0) + jnp.where(lane == i2, w2 * p_group, 0.0)
    return gates + jnp.where(lane == GROUP_ID_LANE, gi, 0.0)


def _mixout_kernel(o_ref, w_ref, x_ref, g1_ref, sh_ref, sc_ref, gf_ref, wr_ref, x1_ref, h2_ref, gates_ref, *,
                   precise):
    y = _mm(o_ref[...], w_ref[...], precise)
    x1 = x_ref[...] + g1_ref[...] * y
    x1_ref[...] = x1
    h2 = _rms_mod(x1, gf_ref[...], sh_ref[...], sc_ref[...])
    h2_ref[...] = h2.astype(h2_ref.dtype)
    if precise:
        logits = _mm(h2, wr_ref[0], True)
    else:
        h_hi = h2.astype(BF16)
        h_lo = (h2 - h_hi.astype(F32)).astype(BF16)
        logits = _dot(h_hi, wr_ref[0]) + (_dot(h_lo, wr_ref[0]) + _dot(h_hi, wr_ref[1]))
    gates_ref[...] = _route(logits)


def _mixout_call(o, w_out, x, mod, layer, g_ffn, w_router, tm, precise):
    bx, lx, d = x.shape
    kd = o.shape[-1]
    return pl.pallas_call(
        functools.partial(_mixout_kernel, precise=precise),
        grid=(bx, lx // tm),
        in_specs=[
            _tok_spec(tm, kd),
            _const_spec((kd, d)),
            _tok_spec(tm, d),
            mod.spec(layer, 2), mod.spec(layer, 3), mod.spec(layer, 4),
            _const_spec((1, d)),
            _const_spec((2, d, LANES)),
        ],
        out_specs=[_tok_spec(tm, d), _tok_spec(tm, d), _tok_spec(tm, LANES)],
        out_shape=[jax.ShapeDtypeStruct((bx, lx, d), F32),
                   jax.ShapeDtypeStruct((bx, lx, d), F32 if precise else BF16),
                   jax.ShapeDtypeStruct((bx, lx, LANES), F32)],
        compiler_params=_cparams(2),
        name="mix_out_router",
    )(o, w_out, x, mod.arr, mod.arr, mod.arr, g_ffn, w_router)


def _moe_kernel(h_ref, gates_ref, wg_ref, wu_ref, wd_ref, x1_ref, g2_ref, gfin_ref, out_ref, acc_ref, *, final,
                precise):
    e = pl.program_id(2)

    @pl.when(e == 0)
    def _():
        acc_ref[...] = jnp.zeros_like(acc_ref)

    h = h_ref[...]
    a = _mm(h, wg_ref[...], precise)
    u = _mm(h, wu_ref[...], precise)
    lane = lax.broadcasted_iota(jnp.int32, gates_ref.shape, 1)
    ge = jnp.sum(jnp.where(lane == e, gates_ref[...], 0.0), axis=-1, keepdims=True)
    act = (_silu(a) * u) * ge
    acc_ref[...] += _mm(act, wd_ref[...], precise)

    @pl.when(e == pl.num_programs(2) - 1)
    def _():
        x2 = x1_ref[...] + g2_ref[...] * acc_ref[...]
        if final:
            x2 = (x2 * lax.rsqrt(jnp.mean(x2 * x2, axis=-1, keepdims=True) + EPS)) * gfin_ref[...]
        out_ref[...] = x2


def _moe_call(h2, gates, wg, wu, wd, layer, x1, mod, g_final, tm, final, precise):
    bx, lx, d = x1.shape
    tok3 = lambda w: pl.BlockSpec((None, tm, w), lambda b, i, e: (b, i, 0))
    g2_spec2 = mod.spec(layer, 5)
    g2_spec = pl.BlockSpec(g2_spec2.block_shape, lambda b, i, e: g2_spec2.index_map(b, i))
    return pl.pallas_call(
        functools.partial(_moe_kernel, final=final, precise=precise),
        grid=(bx, lx // tm, N_EXPERTS),
        in_specs=[
            tok3(d), tok3(LANES),
            pl.BlockSpec((None, d, D_EXPERT), lambda b, i, e: (layer, 0, e)),
            pl.BlockSpec((None, d, D_EXPERT), lambda b, i, e: (layer, 0, e)),
            pl.BlockSpec((None, D_EXPERT, d), lambda b, i, e: (layer, e, 0)),
            tok3(d),
            g2_spec,
            pl.BlockSpec((1, d), lambda b, i, e: (0, 0)),
        ],
        out_specs=tok3(d),
        out_shape=jax.ShapeDtypeStruct((bx, lx, d), F32),
        scratch_shapes=[pltpu.VMEM((tm, d), F32)],
        compiler_params=_cparams(3),
        name="hier_moe",
    )(h2, gates, wg, wu, wd, x1, mod.arr, g_final)


def _moe_plan_kernel(gates_ref, pos_ref, gs_ref, flags_ref, *, chunk, rows):
    g = gates_ref[...]
    tm = g.shape[0]
    lane = lax.broadcasted_iota(jnp.int32, (tm, LANES), 1)
    gid = jnp.sum(jnp.where(lane == GROUP_ID_LANE, g, 0.0), axis=-1, keepdims=True)
    onehot = jnp.where((lane.astype(F32) == gid) & (lane < N_GROUPS), 1.0, 0.0)
    r = lax.broadcasted_iota(jnp.int32, (tm, tm), 0)
    c = lax.broadcasted_iota(jnp.int32, (tm, tm), 1)
    before = _dot((c < r).astype(BF16), onehot.astype(BF16))
    rank = jnp.sum(onehot * before, axis=-1, keepdims=True)
    count = jnp.broadcast_to(jnp.sum(onehot, axis=0, keepdims=True), (8, LANES))
    lower = (lax.broadcasted_iota(jnp.int32, (LANES, LANES), 0)
             < lax.broadcasted_iota(jnp.int32, (LANES, LANES), 1)).astype(BF16)
    hi, mid, lo = _split3(jnp.ceil(count / chunk) * chunk)
    start = (_dot(hi, lower) + _dot(mid, lower)) + _dot(lo, lower)
    pos = jnp.sum(onehot * start[0:1], axis=-1, keepdims=True) + rank
    pos_ref[...] = jnp.broadcast_to(pos, (tm, LANES))
    dest = lax.broadcasted_iota(jnp.int32, (tm, rows), 1).astype(F32)
    place = (pos == dest).astype(BF16)
    parts = jnp.concatenate(_split3(jnp.where(lane < N_EXPERTS, g, 0.0)), axis=1)
    sorted_parts = _dot_tn(place, parts)
    gs = (sorted_parts[:, :LANES] + sorted_parts[:, LANES:2 * LANES]) + sorted_parts[:, 2 * LANES:]
    gs_ref[...] = gs
    flags_ref[...] = jnp.zeros_like(flags_ref)
    for k in range(rows // chunk):
        used = jnp.max(jnp.where(gs[k * chunk:(k + 1) * chunk] != 0.0, 1.0, 0.0), axis=0, keepdims=True)
        flags_ref[k:k + 1, :] = used.astype(jnp.int32)


def _moe_plan_call(gates, tm, chunk, rows):
    bx, lx, _ = gates.shape
    flag_rows = -(-(rows // chunk) // 8) * 8
    tok = lambda: pl.BlockSpec((None, tm, LANES), lambda b, i: (b, i, 0))
    per_tile = lambda r: pl.BlockSpec((None, None, r, LANES), lambda b, i: (b, i, 0, 0))
    return pl.pallas_call(
        functools.partial(_moe_plan_kernel, chunk=chunk, rows=rows),
        grid=(bx, lx // tm),
        in_specs=[tok()],
        out_specs=[tok(), per_tile(rows), per_tile(flag_rows)],
        out_shape=[jax.ShapeDtypeStruct((bx, lx, LANES), F32),
                   jax.ShapeDtypeStruct((bx, lx // tm, rows, LANES), F32),
                   jax.ShapeDtypeStruct((bx, lx // tm, flag_rows, LANES), jnp.int32)],
        compiler_params=_cparams(2),
        name="moe_plan",
    )(gates)


def _moe_sorted_kernel(first_ref, short_ref, nwin_ref, h_ref, pos_ref, gs_ref, wg_ref, wu_ref, wd_ref, x1_ref, g2_ref,
                       gfin_ref, out_ref, place_ref, xs_ref, gsp_ref, acc_ref, *, final, chunk, window, eps):
    b = pl.program_id(0)
    i = pl.program_id(1)
    step = pl.program_id(2)
    tm = h_ref.shape[0]
    rows_sorted = gs_ref.shape[0]
    tile = b * pl.num_programs(1) + i

    @pl.when(step == 0)
    def _():
        dest = lax.broadcasted_iota(jnp.int32, (tm, rows_sorted), 1).astype(F32)
        place = (pos_ref[:, 0:1] == dest).astype(BF16)
        place_ref[...] = place
        xs_ref[...] = jnp.zeros_like(xs_ref)
        xs_ref[0:rows_sorted, :] = _dot_tn(place, h_ref[...]).astype(BF16)
        gsp_ref[...] = jnp.zeros_like(gsp_ref)
        gsp_ref[0:rows_sorted, :] = gs_ref[...]
        acc_ref[...] = jnp.zeros_like(acc_ref)

    for j in range(eps):
        e = step * eps + j
        cols = slice(j * D_EXPERT, (j + 1) * D_EXPERT)

        def run(first_chunk, n_rows, e=e, cols=cols):
            rows = pl.ds(pl.multiple_of(first_chunk * chunk, chunk), n_rows)
            x = xs_ref[rows, :]
            lane = lax.broadcasted_iota(jnp.int32, (n_rows, LANES), 1)
            ge = jnp.sum(jnp.where(lane == e, gsp_ref[rows, :], 0.0), axis=-1, keepdims=True)
            act = (_silu(_dot(x, wg_ref[:, cols])) * _dot(x, wu_ref[:, cols])) * ge
            acc_ref[rows, :] += _dot(act.astype(BF16), wd_ref[cols, :])

        @pl.when(short_ref[tile, e] != 0)
        def _(run=run, e=e):
            run(first_ref[tile, e], (window - 1) * chunk)

        def one_window(w, carry, run=run, e=e):
            run(first_ref[tile, e] + w * window, window * chunk)
            return carry

        lax.fori_loop(0, nwin_ref[tile, e], one_window, 0)

    @pl.when(step == pl.num_programs(2) - 1)
    def _():
        x2 = x1_ref[...] + g2_ref[...] * _dot(place_ref[...], acc_ref[0:rows_sorted, :].astype(BF16))
        if final:
            x2 = (x2 * lax.rsqrt(jnp.mean(x2 * x2, axis=-1, keepdims=True) + EPS)) * gfin_ref[...]
        out_ref[...] = x2


def _moe_sorted_call(h2, gates, wg, wu, wd, layer, x1, mod, g_final, tm, final):
    bx, lx, d = x1.shape
    chunk = min(tm, MOE_CHUNK)
    window = MOE_WINDOW
    rows = tm + N_GROUPS * chunk
    n_chunks = rows // chunk
    pos, gs, flags = _moe_plan_call(gates, tm, chunk, rows)
    used = flags[:, :, :n_chunks, :N_EXPERTS].reshape(-1, n_chunks, N_EXPERTS) != 0
    idx = jnp.arange(n_chunks, dtype=jnp.int32)[None, :, None]
    first = jnp.min(jnp.where(used, idx, n_chunks), axis=1)
    last = jnp.max(jnp.where(used, idx, -1), axis=1)
    span = jnp.where(last >= 0, last - first + 1, 0)
    short = ((span > 0) & (span < window)).astype(jnp.int32)
    nwin = jnp.where(span >= window, (span + window - 1) // window, 0).astype(jnp.int32)
    first = jnp.where(last >= 0, first, 0).astype(jnp.int32)
    pad_rows = (window - 1) * chunk
    tok3 = lambda w: pl.BlockSpec((None, tm, w), lambda b, i, e, *_: (b, i, 0))
    g2_spec2 = mod.spec(layer, 5)
    g2_spec = pl.BlockSpec(g2_spec2.block_shape, lambda b, i, e, *_: g2_spec2.index_map(b, i))
    eps = MOE_EXPERTS_PER_STEP
    grid_spec = pltpu.PrefetchScalarGridSpec(
        num_scalar_prefetch=3,
        grid=(bx, lx // tm, N_EXPERTS // eps),
        in_specs=[
            tok3(d), tok3(LANES),
            pl.BlockSpec((None, None, rows, LANES), lambda b, i, e, *_: (b, i, 0, 0)),
            pl.BlockSpec((None, d, eps * D_EXPERT), lambda b, i, e, *_: (layer, 0, e)),
            pl.BlockSpec((None, d, eps * D_EXPERT), lambda b, i, e, *_: (layer, 0, e)),
            pl.BlockSpec((None, eps * D_EXPERT, d), lambda b, i, e, *_: (layer, e, 0)),
            tok3(d),
            g2_spec,
            pl.BlockSpec((1, d), lambda b, i, e, *_: (0, 0)),
        ],
        out_specs=tok3(d),
        scratch_shapes=[pltpu.VMEM((tm, rows), BF16), pltpu.VMEM((rows + pad_rows, d), BF16),
                        pltpu.VMEM((rows + pad_rows, LANES), F32), pltpu.VMEM((rows + pad_rows, d), F32)],
    )
    return pl.pallas_call(
        functools.partial(_moe_sorted_kernel, final=final, chunk=chunk, window=window, eps=eps),
        grid_spec=grid_spec,
        out_shape=jax.ShapeDtypeStruct((bx, lx, d), F32),
        compiler_params=_cparams(3),
        name="hier_moe_sorted",
    )(first, short, nwin, h2, pos, gs, wg, wu, wd, x1, mod.arr, g_final)


def _log_sigmoid(z):
    return -(jnp.maximum(-z, 0.0) + jnp.log1p(jnp.exp(-jnp.abs(z))))


def _kvq_kernel(x_ref, shk_ref, sck_ref, gkv_ref, wkv_ref, wf_ref, bf_ref, sh1_ref, sc1_ref, gmix_ref, wq_ref,
                k32_ref, v32_ref, k16_ref, v16_ref, lf_ref, q_ref, *, precise, q_scale):
    x = x_ref[...]
    fd = FOX_HEADS * FOX_HD
    n = _rms_mod(x, gkv_ref[...], shk_ref[...], sck_ref[...])
    if not precise:
        n = n.astype(BF16)
    k = _mm(n, wkv_ref[:, 0:fd], precise)
    k32_ref[...] = k
    k16_ref[...] = k.astype(BF16)
    v = _mm(n, wkv_ref[:, fd:2 * fd], precise)
    v32_ref[...] = v
    v16_ref[...] = v.astype(BF16)
    z = _mm(n, wf_ref[...], precise) + bf_ref[...]
    lf_ref[...] = _log_sigmoid(z)[:, :FOX_HEADS]
    h = _rms_mod(x, gmix_ref[...], sh1_ref[...], sc1_ref[...])
    q_ref[...] = (_mm(h, wq_ref[...], precise) * q_scale).astype(q_ref.dtype)


def _kvq_call(x, kvmod, mod, g_kv, w_kv, w_f, b_f, g_mix, w_q, tm, precise, q_scale):
    bx, lx, d = x.shape
    fd = FOX_HEADS * FOX_HD
    return pl.pallas_call(
        functools.partial(_kvq_kernel, precise=precise, q_scale=q_scale),
        grid=(bx, lx // tm),
        in_specs=[
            _tok_spec(tm, d),
            kvmod.spec(0, 0), kvmod.spec(0, 1),
            _const_spec((1, d)),
            _const_spec((d, 2 * fd)),
            _const_spec((d, LANES)),
            _const_spec((1, LANES)),
            mod.spec(1, 0), mod.spec(1, 1),
            _const_spec((1, d)),
            _const_spec((d, fd)),
        ],
        out_specs=[_tok_spec(tm, fd), _tok_spec(tm, fd), _tok_spec(tm, fd), _tok_spec(tm, fd),
                   _tok_spec(tm, FOX_HEADS), _tok_spec(tm, fd)],
        out_shape=[jax.ShapeDtypeStruct((bx, lx, fd), F32), jax.ShapeDtypeStruct((bx, lx, fd), F32),
                   jax.ShapeDtypeStruct((bx, lx, fd), BF16), jax.ShapeDtypeStruct((bx, lx, fd), BF16),
                   jax.ShapeDtypeStruct((bx, lx, FOX_HEADS), F32),
                   jax.ShapeDtypeStruct((bx, lx, fd), F32 if precise else BF16)],
        compiler_params=_cparams(2),
        name="kv_q_proj",
    )(x, kvmod.arr, kvmod.arr, g_kv, w_kv, w_f, b_f, mod.arr, mod.arr, g_mix, w_q)


def _cumsum_kernel(x_ref, o_ref):
    nh, l = x_ref.shape
    r = lax.broadcasted_iota(jnp.int32, (LANES, LANES), 0)
    c = lax.broadcasted_iota(jnp.int32, (LANES, LANES), 1)
    upper = (r <= c).astype(BF16)
    carry = jnp.zeros((nh, 1), F32)
    for blk in range(l // LANES):
        hi, mid, lo = _split3(x_ref[:, blk * LANES:(blk + 1) * LANES])
        cs = (_dot(hi, upper) + _dot(mid, upper)) + _dot(lo, upper) + carry
        o_ref[:, blk * LANES:(blk + 1) * LANES] = cs
        carry = cs[:, LANES - 1:LANES]


def _cumsum_call(x):
    b, nh, l = x.shape
    return pl.pallas_call(
        _cumsum_kernel,
        grid=(b,),
        in_specs=[pl.BlockSpec((None, nh, l), lambda i: (i, 0, 0))],
        out_specs=pl.BlockSpec((None, nh, l), lambda i: (i, 0, 0)),
        out_shape=jax.ShapeDtypeStruct((b, nh, l), F32),
        compiler_params=_cparams(1),
        name="logf_cumsum",
    )(x)


def _fox_kernel(q_ref, k_ref, v_ref, fq_ref, fk_ref, o_ref, acc_ref, *, t):
    qi = pl.program_id(2)
    first = lax.broadcasted_iota(jnp.int32, (1, LANES), 1) < FOX_HD
    col_a = lax.broadcasted_iota(jnp.int32, (1, 2 * t), 1) < t
    q2 = q_ref[...]
    zero = jnp.zeros_like(q2)
    qs = jnp.concatenate([jnp.where(first, q2, zero), jnp.where(first, zero, q2)], axis=0)
    fq = jnp.concatenate([fq_ref[0:1, :], fq_ref[1:2, :]], axis=1) * LOG2E
    key = lax.broadcasted_iota(jnp.int32, (t, 2 * t), 0)
    qry = lax.broadcasted_iota(jnp.int32, (t, 2 * t), 1)
    causal = key <= jnp.where(col_a, qry, qry - t)
    acc_ref[...] = jnp.zeros_like(acc_ref)

    def tile(j, m, l, diagonal):
        start = pl.multiple_of(j * t, t)
        kt = k_ref[pl.ds(start, t), :]
        vt = v_ref[pl.ds(start, t), :]
        fk = fk_ref[pl.ds(start, t), :] * LOG2E
        s = (_dot_nt(kt, qs) + fq) - jnp.where(col_a, fk[:, 0:1], fk[:, 1:2])
        if diagonal:
            s = jnp.where(causal, s, NEG_INF)
        m_new = jnp.maximum(m, jnp.max(s, axis=0, keepdims=True))
        p = jnp.exp2(s - m_new)
        alpha = jnp.exp2(m - m_new)
        acc_ref[...] = alpha * acc_ref[...] + _dot_tn(vt, p.astype(BF16))
        return m_new, alpha * l + jnp.sum(p, axis=0, keepdims=True)

    m0 = jnp.full((1, 2 * t), -jnp.inf, F32)
    l0 = jnp.zeros((1, 2 * t), F32)
    m, l = lax.fori_loop(0, qi, lambda j, c: tile(j, c[0], c[1], False), (m0, l0))

    m, l = tile(qi, m, l, True)
    o_t = acc_ref[...] / l
    row_a = lax.broadcasted_iota(jnp.int32, (LANES, 1), 0) < FOX_HD
    o_ref[...] = jnp.where(row_a, o_t[:, :t], o_t[:, t:]).T.astype(BF16)


def _fox_prompt_call(q, k16, v16, f_rows, f_cols, tq):
    b, l, fd = q.shape
    npair = fd // LANES
    return pl.pallas_call(
        functools.partial(_fox_kernel, t=tq),
        grid=(b, npair, l // tq),
        in_specs=[
            pl.BlockSpec((None, tq, LANES), lambda bb, hp, i: (bb, i, hp)),
            pl.BlockSpec((None, l, LANES), lambda bb, hp, i: (bb, 0, hp)),
            pl.BlockSpec((None, l, LANES), lambda bb, hp, i: (bb, 0, hp)),
            pl.BlockSpec((None, None, None, 2, tq), lambda bb, hp, i: (bb, hp, i, 0, 0)),
            pl.BlockSpec((None, None, l, 2), lambda bb, hp, i: (bb, hp, 0, 0)),
        ],
        out_specs=pl.BlockSpec((None, tq, LANES), lambda bb, hp, i: (bb, i, hp)),
        out_shape=jax.ShapeDtypeStruct((b, l, fd), BF16),
        scratch_shapes=[pltpu.VMEM((LANES, 2 * tq), F32)],
        compiler_params=_cparams(3),
        name="fox_attention_prompt",
    )(q, k16, v16, f_rows, f_cols)


def _attn_sample_kernel(pt_ref, q_ref, kn_ref, vn_ref, lfn_ref, *refs, pps, page):
    ck = refs[0:pps]
    cv = refs[pps:2 * pps]
    clf = refs[2 * pps:3 * pps]
    o_ref = refs[3 * pps]
    m_ref, l_ref, acc_ref, carry_ref = refs[3 * pps + 1:]
    step = pl.program_id(1)
    nh, hd = FOX_HEADS, FOX_HD
    fd = nh * hd
    diag = (lax.broadcasted_iota(jnp.int32, (nh, fd), 1) // hd) == lax.broadcasted_iota(jnp.int32, (nh, fd), 0)
    q_bd = jnp.where(diag, jnp.broadcast_to(q_ref[...], (nh, fd)), 0.0)

    @pl.when(step == 0)
    def _():
        m_ref[...] = jnp.sum(q_bd * kn_ref[...], axis=-1, keepdims=True)
        l_ref[...] = jnp.ones_like(l_ref)
        acc_ref[...] = jnp.broadcast_to(vn_ref[...], (nh, fd))
        carry_ref[...] = lfn_ref[...]

    tt = lax.broadcasted_iota(jnp.int32, (page, 2 * page), 0)
    cc = lax.broadcasted_iota(jnp.int32, (page, 2 * page), 1)
    later = ((tt > cc) | (cc >= page)).astype(BF16)

    def split2(x):
        hi = x.astype(BF16)
        return jnp.concatenate([hi, (x - hi.astype(F32)).astype(BF16)], axis=0)

    q2 = split2(q_bd)
    lf_parts = [part for i in range(pps) for part in _split3(clf[i][...])]
    lfx_all = _dot(jnp.concatenate(lf_parts, axis=0), later)
    carry = carry_ref[...]
    scores = []
    for i in range(pps):
        base = 3 * nh * i
        lfx = (lfx_all[base:base + nh] + lfx_all[base + nh:base + 2 * nh]) + lfx_all[base + 2 * nh:base + 3 * nh]
        bias = lfx[:, :page] + carry
        carry = carry + lfx[:, page:page + 1]
        s2 = _dot(q2, ck[i][...].reshape(fd, page).astype(BF16))
        scores.append((s2[:nh] + s2[nh:]) + bias)
    carry_ref[...] = carry
    s = jnp.concatenate(scores, axis=1)
    m_old = m_ref[...]
    m_new = jnp.maximum(m_old, jnp.max(s, axis=-1, keepdims=True))
    p = jnp.exp(s - m_new)
    alpha = jnp.exp(m_old - m_new)
    m_ref[...] = m_new
    l_ref[...] = alpha * l_ref[...] + jnp.sum(p, axis=-1, keepdims=True)
    p2 = split2(p)
    pv = _dot_nt(p2[:, 0:page], cv[0][...].reshape(fd, page).astype(BF16))
    for i in range(1, pps):
        pv = pv + _dot_nt(p2[:, i * page:(i + 1) * page], cv[i][...].reshape(fd, page).astype(BF16))
    acc_ref[...] = alpha * acc_ref[...] + (pv[:nh] + pv[nh:])

    @pl.when(step == pl.num_programs(1) - 1)
    def _():
        o = acc_ref[...] / l_ref[...]
        o_ref[...] = jnp.sum(jnp.where(diag, o, 0.0), axis=0, keepdims=True)


def _attn_sample_call(page_table, q, k_new, v_new, lf_new, cache_kt, cache_vt, cache_lft, pps):
    nb, n_pages = page_table.shape
    n_pool, nh, hd, page = cache_kt.shape
    fd = nh * hd
    steps = n_pages // pps

    def page_map(i, nd):
        return lambda b, s, pt: (pt[b, n_pages - 1 - (s * pps + i)],) + (0,) * nd

    row = lambda r, w: pl.BlockSpec((None, r, w), lambda b, s, pt: (b, 0, 0))
    in_specs = [row(1, fd), row(1, fd), row(1, fd), row(nh, 1)]
    in_specs += [pl.BlockSpec((None, nh, hd, page), page_map(i, 3)) for i in range(pps)]
    in_specs += [pl.BlockSpec((None, nh, hd, page), page_map(i, 3)) for i in range(pps)]
    in_specs += [pl.BlockSpec((None, nh, page), page_map(i, 2)) for i in range(pps)]
    grid_spec = pltpu.PrefetchScalarGridSpec(
        num_scalar_prefetch=1,
        grid=(nb, steps),
        in_specs=in_specs,
        out_specs=row(1, fd),
        scratch_shapes=[pltpu.VMEM((nh, 1), F32), pltpu.VMEM((nh, 1), F32), pltpu.VMEM((nh, fd), F32),
                        pltpu.VMEM((nh, 1), F32)],
    )
    return pl.pallas_call(
        functools.partial(_attn_sample_kernel, pps=pps, page=page),
        grid_spec=grid_spec,
        out_shape=jax.ShapeDtypeStruct((nb, 1, fd), F32),
        compiler_params=_cparams(2),
        name="fox_attention_sample",
    )(page_table, q, k_new, v_new, lf_new, *([cache_kt] * pps), *([cache_vt] * pps), *([cache_lft] * pps))


def _rope_tables(pos):
    half = RET_DK // 2
    inv = ROPE_BASE ** (-jnp.arange(half, dtype=F32) / half)
    ang = pos.astype(F32)[:, None] * inv[None, :]
    return jnp.cos(ang), jnp.sin(ang)


def _pick_tile(n, pref):
    t = min(n, pref)
    while n % t:
        t //= 2
    return t


def kernel(x_prompt, x_sample, c_prompt, c_sample, state_ret, cache_k, cache_v, cache_logf, page_table, w_mod, b_mod, g_mix, g_ffn, w_ret_in, w_ret_out, g_kv, w_kvmod, b_kvmod, w_kv, w_f, b_f, w_fq, w_fo, w_rg, w_re, w_e_gate, w_e_up, w_e_down, g_final):
    b, l, d = x_prompt.shape
    nb = x_sample.shape[0]
    n_pool, page = cache_k.shape[0], cache_k.shape[1]
    n_pages = page_table.shape[1]
    past_len = n_pages * page
    fd = FOX_HEADS * FOX_HD

    c_all = jnp.concatenate([c_sample, c_prompt], axis=0)
    mod_all = _mod_call(c_all, w_mod, b_mod, 1536)
    kvmod_all = _mod_call(c_all, w_kvmod[None], b_kvmod[None], 1024)

    w_f_pad = jnp.pad(w_f, ((0, 0), (0, LANES - FOX_HEADS)))
    weights32 = dict(w_in=w_ret_in, w_out=w_ret_out[0], w_kv=w_kv, w_fq=w_fq[0], w_fo=w_fo[0], w_f=w_f_pad,
                     wg=w_e_gate, wu=w_e_up, wd=w_e_down)
    weights16 = {name: w.astype(BF16) for name, w in weights32.items()}
    b_f_row = jnp.pad(b_f, (0, LANES - FOX_HEADS)).reshape(1, LANES)
    w_r = jnp.pad(jnp.concatenate([w_re, w_rg], axis=-1), ((0, 0), (0, 0), (0, LANES - N_EXPERTS - N_GROUPS)))
    w_r_hi = w_r.astype(BF16)
    routers32 = jnp.stack([w_r, jnp.zeros_like(w_r)], axis=1)
    routers16 = jnp.stack([w_r_hi, (w_r - w_r_hi.astype(F32)).astype(BF16)], axis=1)
    g_mix2 = g_mix.reshape(-1, 1, d)
    g_ffn2 = g_ffn.reshape(-1, 1, d)
    g_kv2 = g_kv.reshape(1, d)
    g_fin2 = g_final.reshape(1, d)

    def trunk(x, mod, kvmod, pos, tm, tm_in, tm_moe, mixer0, mixer1, precise, q_scale):
        w = weights32 if precise else weights16
        cos, sin = _rope_tables(pos)
        proj = _retin_call(x, mod, g_mix2[0], w["w_in"], cos, sin, tm_in, precise)
        o, s_new = mixer0(proj)
        def moe(h, gates, layer, x_res, final):
            if precise:
                return _moe_call(h, gates, w["wg"], w["wu"], w["wd"], layer, x_res, mod, g_fin2, tm_moe, final, True)
            return _moe_sorted_call(h, gates, w["wg"], w["wu"], w["wd"], layer, x_res, mod, g_fin2, tm_moe, final)

        w_router = routers32 if precise else routers16
        x1, h2, gates = _mixout_call(o, w["w_out"], x, mod, 0, g_ffn2[0], w_router[0], tm, precise)
        x2 = moe(h2, gates, 0, x1, False)
        k32, v32, k16, v16, lf, qf = _kvq_call(x2, kvmod, mod, g_kv2, w["w_kv"], w["w_f"], b_f_row, g_mix2[1],
                                               w["w_fq"], tm, precise, q_scale)
        o1 = mixer1(qf, k32, v32, k16, v16, lf)
        x3, h4, gates1 = _mixout_call(o1, w["w_fo"], x2, mod, 1, g_ffn2[1], w_router[1], tm, precise)
        y = moe(h4, gates1, 1, x3, True)
        return y, s_new, k32, v32, lf

    tm_p = _pick_tile(l, 512)
    chunk = _pick_tile(l, 256)
    tq = _pick_tile(l, 512)

    def ret_prompt(proj):
        return _ret_prompt_call(proj, chunk)

    def fox_prompt(qf, k32, v32, k16, v16, lf):
        f_t = _cumsum_call(jnp.swapaxes(lf, 1, 2))
        f_pairs = f_t.reshape(b, FOX_HEADS // 2, 2, l)
        f_cols = jnp.swapaxes(f_pairs, 2, 3)
        f_rows = jnp.swapaxes(f_pairs.reshape(b, FOX_HEADS // 2, 2, l // tq, tq), 2, 3)
        return _fox_prompt_call(qf, k16, v16, f_rows, f_cols, tq)

    mod_p = _Mod(mod_all, False, nb, tm_p)
    kvmod_p = _Mod(kvmod_all, False, nb, tm_p)
    y_p, s_p, k_p, v_p, lf_p = trunk(x_prompt, mod_p, kvmod_p, jnp.arange(l), tm_p, _pick_tile(l, 2048),
                                     _pick_tile(l, 1024), ret_prompt, fox_prompt, False, FOX_HD ** -0.5 * LOG2E)

    def ret_sample(proj):
        o, s_new = _ret_sample_call(proj[0], state_ret[0])
        return o.reshape(1, nb, -1), s_new

    def fox_sample(qf, k32, v32, k16, v16, lf):
        rows = lambda a: a.reshape(nb, 1, fd)
        o = _attn_sample_call(page_table, rows(qf), rows(k32), rows(v32), lf.reshape(nb, FOX_HEADS, 1),
                              jnp.transpose(cache_k, (0, 2, 3, 1)), jnp.transpose(cache_v, (0, 2, 3, 1)),
                              jnp.swapaxes(cache_logf, 1, 2), _pick_tile(n_pages, 8))
        return o.reshape(1, nb, fd)

    mod_s = _Mod(mod_all, True, 0, nb)
    kvmod_s = _Mod(kvmod_all, True, 0, nb)
    y_s, s_s, k_s, v_s, lf_s = trunk(x_sample.reshape(1, nb, d), mod_s, kvmod_s,
                                     jnp.full((nb,), past_len, jnp.int32), nb, nb, nb, ret_sample, fox_sample, True,
                                     FOX_HD ** -0.5)

    return (y_p, y_s.reshape(nb, 1, d), s_p[None], s_s[None],
            k_p.reshape(b, l, FOX_HEADS, FOX_HD), v_p.reshape(b, l, FOX_HEADS, FOX_HD), lf_p,
            k_s.reshape(nb, 1, FOX_HEADS, FOX_HD), v_s.reshape(nb, 1, FOX_HEADS, FOX_HD),
            lf_s.reshape(nb, 1, FOX_HEADS))
```
